```python
import jax
import jax.numpy as jnp
from jax import lax
import numpy as np

D_MODEL = 4096
BATCH = 2
SEQ = 4096
DEPTH = 2
DEC_BATCH = 4
DEC_SEQ = 2048
PAST_LEN = 128

HEAD_DIM = 128
MIX_WIDTH = D_MODEL
RET_HEADS = MIX_WIDTH // (2 * HEAD_DIM)
ATT_HEADS = MIX_WIDTH // (2 * HEAD_DIM)
RET_WIDTH = RET_HEADS * HEAD_DIM
ATT_WIDTH = ATT_HEADS * HEAD_DIM
IN_SPLITS = (RET_WIDTH, RET_WIDTH, RET_WIDTH, RET_WIDTH, ATT_WIDTH, ATT_WIDTH, ATT_WIDTH)
IN_COLS = sum(IN_SPLITS)
RET_CHUNK = 128
DILATED_BRANCHES = ((128, 1), (512, 4), (2048, 16))
D_FF = -(-8 * D_MODEL // (3 * 256)) * 256
PLE_DIM = 256
RMS_EPS = 1e-6
GN_EPS = 1e-5
NEG_BIG = -1e30

kernel_name = 'hybrid_retention_dilated_encoder'


def rms_norm(x, g):
    xf = x.astype(jnp.float32)
    y = xf * lax.rsqrt(jnp.mean(xf * xf, axis=-1, keepdims=True) + RMS_EPS)
    return (y * g.astype(jnp.float32)).astype(x.dtype)


def alibi_slopes(n):
    return jnp.asarray(2.0 ** (-8.0 * np.arange(1, n + 1) / n), jnp.float32)


def retention_direction(q, k, v, log_gamma, strict):
    b, l, h, dh = q.shape
    c = RET_CHUNK
    n = l // c
    dt = q.dtype
    qc = q.reshape(b, n, c, h, dh)
    kc = k.reshape(b, n, c, h, dh)
    vc = v.reshape(b, n, c, h, dh)
    pos = jnp.arange(c, dtype=jnp.float32)
    diff = pos[:, None] - pos[None, :]
    valid = (diff > 0) if strict else (diff >= 0)
    intra_decay = jnp.where(valid[None], jnp.exp(log_gamma[:, None, None] * jnp.maximum(diff, 0.0)[None]), 0.0).astype(dt)
    scores = jnp.einsum('bnihd,bnjhd->bnhij', qc, kc) * intra_decay[None, None]
    intra = jnp.einsum('bnhij,bnjhd->bnihd', scores, vc)
    zeta = jnp.exp(log_gamma[None, :] * (c - 1 - pos)[:, None]).astype(dt)
    kv = jnp.einsum('bnjhd,jh,bnjhe->nbhde', kc, zeta, vc)
    chunk_decay = jnp.exp(log_gamma * c).astype(dt)[None, :, None, None]

    def step(state, kv_n):
        return state * chunk_decay + kv_n, state

    _, prev = lax.scan(step, jnp.zeros_like(kv[0]), kv)
    xi = jnp.exp(log_gamma[None, :] * (pos + 1)[:, None]).astype(dt)
    cross = jnp.einsum('bnihd,nbhde,ih->bnihe', qc, prev, xi)
    return (intra + cross).reshape(b, l, h, dh)


def retention_mixer(q, k, v, g, log_decay, gn_gain):
    b, l, h, dh = q.shape
    k = k * (dh ** -0.5)
    ld = log_decay.astype(jnp.float32)
    fwd = retention_direction(q, k, v, ld[0], False)
    bwd = retention_direction(q[:, ::-1], k[:, ::-1], v[:, ::-1], ld[1], True)[:, ::-1]
    y = (fwd + bwd).astype(jnp.float32)
    mu = jnp.mean(y, axis=-1, keepdims=True)
    var = jnp.mean(jnp.square(y - mu), axis=-1, keepdims=True)
    y = ((y - mu) * lax.rsqrt(var + GN_EPS)).reshape(b, l, h * dh) * gn_gain.astype(jnp.float32)
    return (jax.nn.silu(g.astype(jnp.float32)) * y).astype(q.dtype)


def dilated_branch(q, k, v, slopes, dilation, half):
    b, l, h, dh = q.shape
    m = l // dilation
    xb = b * dilation

    def to_residue(t):
        return jnp.swapaxes(t.reshape(b, m, dilation, h, dh), 1, 2).reshape(xb, m, h, dh)

    qr, kr, vr = to_residue(q), to_residue(k), to_residue(v)
    w = half
    nb = -(-m // w)
    mp = nb * w
    qb = jnp.pad(qr, ((0, 0), (0, mp - m), (0, 0), (0, 0))).reshape(xb, nb, w, h, dh)
    kp = jnp.pad(kr, ((0, 0), (w, mp - m + w), (0, 0), (0, 0)))
    vp = jnp.pad(vr, ((0, 0), (w, mp - m + w), (0, 0), (0, 0)))
    key_idx = w * jnp.arange(nb)[:, None] + jnp.arange(3 * w)[None, :]
    kb = kp[:, key_idx]
    vb = vp[:, key_idx]
    s = jnp.einsum('xnihd,xnjhd->xnhij', qb, kb).astype(jnp.float32)
    rel = jnp.arange(3 * w)[None, :] - w - jnp.arange(w)[:, None]
    key_pos = key_idx - w
    in_range = (key_pos >= 0) & (key_pos < m)
    valid = (jnp.abs(rel) <= w)[None] & in_range[:, None, :]
    bias = -slopes[:, None, None] * (dilation * jnp.abs(rel)).astype(jnp.float32)[None]
    s = jnp.where(valid[None, :, None], s + bias[None, None], NEG_BIG)
    mx = jnp.max(s, axis=-1, keepdims=True)
    e = jnp.exp(s - mx)
    den = jnp.sum(e, axis=-1, keepdims=True)
    out = jnp.einsum('xnhij,xnjhd->xnihd', (e / den).astype(v.dtype), vb)
    lse = (mx + jnp.log(den))[..., 0]
    out = out.reshape(xb, mp, h, dh)[:, :m]
    lse = jnp.swapaxes(lse, 2, 3).reshape(xb, mp, h)[:, :m]
    out = jnp.swapaxes(out.reshape(b, dilation, m, h, dh), 1, 2).reshape(b, l, h, dh)
    lse = jnp.swapaxes(lse.reshape(b, dilation, m, h), 1, 2).reshape(b, l, h)
    return out, lse


def dilated_attention(q, k, v, q_gain, k_gain):
    q = rms_norm(q, q_gain) * (HEAD_DIM ** -0.5)
    k = rms_norm(k, k_gain)
    slopes = alibi_slopes(q.shape[2])
    outs, lses = [], []
    for window, dil in DILATED_BRANCHES:
        o, lse = dilated_branch(q, k, v, slopes, dil, window // (2 * dil))
        outs.append(o)
        lses.append(lse)
    wts = jax.nn.softmax(jnp.stack(lses, axis=0), axis=0)
    return jnp.einsum('rblh,rblhd->blhd', wts.astype(v.dtype), jnp.stack(outs, axis=0))


def encoder_layer(x, p_i, ln_mix, w_in, ret_log_decay, ret_gn, q_norm, k_norm, w_out,
                  ln_ffn, w_gate, w_up, w_down, ln_ple, w_ple_gate, w_ple_proj):
    b, l, _ = x.shape
    u = rms_norm(x, ln_mix)
    z = u @ w_in
    cuts = [int(c) for c in np.cumsum(IN_SPLITS)[:-1]]
    rq, rk, rv, rg, aq, ak, av = jnp.split(z, cuts, axis=-1)
    rh = lambda t: t.reshape(b, l, RET_HEADS, HEAD_DIM)
    ah = lambda t: t.reshape(b, l, ATT_HEADS, HEAD_DIM)
    ret = retention_mixer(rh(rq), rh(rk), rh(rv), rg, ret_log_decay, ret_gn)
    att = dilated_attention(ah(aq), ah(ak), ah(av), q_norm, k_norm).reshape(b, l, ATT_WIDTH)
    x = x + jnp.concatenate([ret, att], axis=-1) @ w_out
    f = rms_norm(x, ln_ffn)
    x = x + (jax.nn.silu(f @ w_gate) * (f @ w_up)) @ w_down
    e = rms_norm(x, ln_ple)
    x = x + jax.nn.sigmoid(e @ w_ple_gate) * (p_i @ w_ple_proj)
    return x


def run_trunk(x, p, ln_mix, w_in, ret_log_decay, ret_gn, q_norm, k_norm, w_out,
              ln_ffn, w_gate, w_up, w_down, ln_ple, w_ple_gate, w_ple_proj):
    for i in range(DEPTH):
        x = encoder_layer(x, p[i], ln_mix[i], w_in[i], ret_log_decay[i], ret_gn[i], q_norm[i], k_norm[i],
                          w_out[i], ln_ffn[i], w_gate[i], w_up[i], w_down[i], ln_ple[i], w_ple_gate[i],
                          w_ple_proj[i])
    return x


def setup_inputs(seed: int = 0) -> dict:
    key = jax.random.key(seed)
    ks = jax.random.split(key, 20)
    f32 = jnp.float32

    def normal(k, shape, scale=1.0):
        return scale * jax.random.normal(k, shape, f32)

    def gain(k, shape):
        return 1.0 + 0.02 * jax.random.normal(k, shape, f32)

    base_decay = jnp.asarray(np.log1p(-2.0 ** (-5.0 - np.arange(RET_HEADS))), f32)
    ret_log_decay = base_decay * (1.0 + 0.05 * jax.random.normal(ks[6], (DEPTH, 2, RET_HEADS), f32))
    return {
        'x_prompt': normal(ks[0], (BATCH, SEQ, D_MODEL)),
        'x_sample': normal(ks[1], (DEC_BATCH, DEC_SEQ, D_MODEL)),
        'p_prompt': normal(ks[2], (DEPTH, BATCH, SEQ, PLE_DIM)),
        'p_sample': normal(ks[3], (DEPTH, DEC_BATCH, DEC_SEQ, PLE_DIM)),
        'ln_mix': gain(ks[4], (DEPTH, D_MODEL)),
        'w_in': normal(ks[5], (DEPTH, D_MODEL, IN_COLS), D_MODEL ** -0.5),
        'ret_log_decay': ret_log_decay,
        'ret_gn': gain(ks[7], (DEPTH, RET_WIDTH)),
        'q_norm': gain(ks[8], (DEPTH, HEAD_DIM)),
        'k_norm': gain(ks[9], (DEPTH, HEAD_DIM)),
        'w_out': normal(ks[10], (DEPTH, MIX_WIDTH, D_MODEL), MIX_WIDTH ** -0.5),
        'ln_ffn': gain(ks[11], (DEPTH, D_MODEL)),
        'w_gate': normal(ks[12], (DEPTH, D_MODEL, D_FF), D_MODEL ** -0.5),
        'w_up': normal(ks[13], (DEPTH, D_MODEL, D_FF), D_MODEL ** -0.5),
        'w_down': normal(ks[14], (DEPTH, D_FF, D_MODEL), D_FF ** -0.5),
        'ln_ple': gain(ks[15], (DEPTH, D_MODEL)),
        'w_ple_gate': normal(ks[16], (DEPTH, D_MODEL, D_MODEL), D_MODEL ** -0.5),
        'w_ple_proj': normal(ks[17], (DEPTH, PLE_DIM, D_MODEL), PLE_DIM ** -0.5),
    }


def reference(x_prompt, x_sample, p_prompt, p_sample, ln_mix, w_in, ret_log_decay, ret_gn, q_norm,
              k_norm, w_out, ln_ffn, w_gate, w_up, w_down, ln_ple, w_ple_gate, w_ple_proj):
    y_prompt = run_trunk(x_prompt, p_prompt, ln_mix, w_in, ret_log_decay, ret_gn, q_norm, k_norm, w_out,
                         ln_ffn, w_gate, w_up, w_down, ln_ple, w_ple_gate, w_ple_proj)
    y_sample = run_trunk(x_sample, p_sample, ln_mix, w_in, ret_log_decay, ret_gn, q_norm, k_norm, w_out,
                         ln_ffn, w_gate, w_up, w_down, ln_ple, w_ple_gate, w_ple_proj)
    return (y_prompt, y_sample)
```

```python
import functools

import jax
import jax.numpy as jnp
import numpy as np
from jax import lax
from jax.experimental import pallas as pl
from jax.experimental.pallas import tpu as pltpu

D_MODEL = 4096
HEAD_DIM = 128
N_HEADS = 16
MIX_HALF = N_HEADS * HEAD_DIM
D_FF = 11008
D_FF_PAD = 11264
PLE_DIM = 256
RMS_EPS = 1e-6
GN_EPS = 1e-5
NEG_BIG = -1e30
RET_CHUNK = 128
BRANCH_DILATIONS = (1, 4, 16)
BRANCH_HALF = 64
LANES = 128
ATT_BLOCK = 128

F32 = jnp.float32
BF16 = jnp.bfloat16
VMEM_HEADROOM = 6 * 1024 * 1024


def _nbytes(shape, dtype):
    return int(np.prod(shape)) * jnp.dtype(dtype).itemsize


def _params(block_bytes, scratch_bytes=0, n_grid=2):
    limit = 2 * block_bytes + scratch_bytes + VMEM_HEADROOM
    return pltpu.CompilerParams(
        dimension_semantics=("arbitrary",) * n_grid,
        vmem_limit_bytes=int(limit),
    )


def _row_scale(ssq_ref):
    tot = jnp.sum(ssq_ref[...], axis=-1, keepdims=True)
    return lax.rsqrt(tot * (1.0 / D_MODEL) + RMS_EPS)


def _lane_partial_ssq(x):
    sq = x * x
    acc = sq[:, 0:LANES]
    for c in range(1, x.shape[1] // LANES):
        acc = acc + sq[:, c * LANES:(c + 1) * LANES]
    return acc


def _prep_kernel(x_ref, xb_ref, ssq_ref):
    x = x_ref[...]
    xb_ref[...] = x.astype(BF16)
    ssq_ref[...] = _lane_partial_ssq(x)


def _prep(x):
    m = x.shape[0]
    tm = 512
    blocks = _nbytes((tm, D_MODEL), F32) + _nbytes((tm, D_MODEL), BF16) + _nbytes((tm, LANES), F32)
    return pl.pallas_call(
        _prep_kernel,
        grid=(m // tm,),
        in_specs=[pl.BlockSpec((tm, D_MODEL), lambda i: (i, 0))],
        out_specs=[pl.BlockSpec((tm, D_MODEL), lambda i: (i, 0)),
                   pl.BlockSpec((tm, LANES), lambda i: (i, 0))],
        out_shape=[jax.ShapeDtypeStruct((m, D_MODEL), BF16),
                   jax.ShapeDtypeStruct((m, LANES), F32)],
        compiler_params=_params(blocks, n_grid=1),
        name="prep",
    )(x)


def _mm_in_kernel(xb_ref, ssq_ref, w_ref, o_ref):
    acc = jnp.dot(xb_ref[...], w_ref[...], preferred_element_type=F32)
    o_ref[...] = (acc * _row_scale(ssq_ref)).astype(o_ref.dtype)


def _mm_in(xb, ssq, w, col0, ncols, out_dtype, tm=1024, tn=512):
    m, k = xb.shape
    s = ssq.shape[1]
    jb0 = col0 // tn
    blocks = (_nbytes((tm, k), BF16) + _nbytes((tm, s), F32) + _nbytes((k, tn), BF16)
              + _nbytes((tm, tn), out_dtype))
    return pl.pallas_call(
        _mm_in_kernel,
        grid=(m // tm, ncols // tn),
        in_specs=[pl.BlockSpec((tm, k), lambda i, j: (i, 0)),
                  pl.BlockSpec((tm, s), lambda i, j: (i, 0)),
                  pl.BlockSpec((k, tn), lambda i, j: (0, j + jb0))],
        out_specs=pl.BlockSpec((tm, tn), lambda i, j: (i, j)),
        out_shape=jax.ShapeDtypeStruct((m, ncols), out_dtype),
        compiler_params=_params(blocks, _nbytes((tm, tn), F32)),
        name="mm_in",
    )(xb, ssq, w)


def _ret_kernel(ld_ref, q_ref, k_ref, v_ref, g_ref, gn_ref, o_ref,
                kvf_s, kvb_s, sf_s, sb_s, *, seq_len):
    c = RET_CHUNK
    n_chunks = seq_len // c
    h = pl.program_id(1)
    lf = ld_ref[0, h]
    lb = ld_ref[1, h]
    scale = HEAD_DIM ** -0.5

    ii = lax.broadcasted_iota(jnp.int32, (c, c), 0)
    jj = lax.broadcasted_iota(jnp.int32, (c, c), 1)
    dist = (ii - jj).astype(F32)
    dmat = jnp.where(ii >= jj,
                     jnp.exp(lf * jnp.maximum(dist, 0.0)),
                     jnp.exp(lb * jnp.maximum(-dist, 0.0))) * scale
    pos = lax.broadcasted_iota(jnp.int32, (c, 1), 0).astype(F32)
    zeta_f = jnp.exp(lf * (c - 1.0 - pos)) * scale
    zeta_b = jnp.exp(lb * pos) * scale
    xi_f = jnp.exp(lf * (pos + 1.0))
    xi_b = jnp.exp(lb * (c - pos))
    one = jnp.ones((1, 1), F32)
    cd_f = jnp.exp(one * (lf * c))
    cd_b = jnp.exp(one * (lb * c))

    tdims = (((0,), (0,)), ((), ()))

    def rows(n):
        return pl.ds(pl.multiple_of(n * c, c), c)

    def summaries(n, carry):
        k = k_ref[rows(n), :]
        v = v_ref[rows(n), :].astype(F32)
        vz = jnp.concatenate([(v * zeta_f).astype(BF16), (v * zeta_b).astype(BF16)], axis=1)
        kv = lax.dot_general(k, vz, tdims, preferred_element_type=F32)
        kvf_s[n] = kv[:, :HEAD_DIM]
        kvb_s[n] = kv[:, HEAD_DIM:]
        return carry

    lax.fori_loop(0, n_chunks, summaries, 0, unroll=2)

    def scan_f(n, st):
        sf_s[n] = st.astype(BF16)
        return st * cd_f + kvf_s[n]

    lax.fori_loop(0, n_chunks, scan_f, jnp.zeros((HEAD_DIM, HEAD_DIM), F32))

    def scan_b(t, st):
        n = n_chunks - 1 - t
        sb_s[n] = st.astype(BF16)
        return st * cd_b + kvb_s[n]

    lax.fori_loop(0, n_chunks, scan_b, jnp.zeros((HEAD_DIM, HEAD_DIM), F32))

    gain = gn_ref[...]

    def outputs(n, carry):
        q = q_ref[rows(n), :]
        k = k_ref[rows(n), :]
        v = v_ref[rows(n), :]
        s = lax.dot_general(q, k, (((1,), (1,)), ((), ())), preferred_element_type=F32)
        p = (s * dmat).astype(BF16)
        intra = jnp.dot(p, v, preferred_element_type=F32)
        st = jnp.concatenate([sf_s[n], sb_s[n]], axis=1)
        cross = jnp.dot(q, st, preferred_element_type=F32)
        y = intra + cross[:, :HEAD_DIM] * xi_f + cross[:, HEAD_DIM:] * xi_b
        mu = jnp.mean(y, axis=-1, keepdims=True)
        yc = y - mu
        var = jnp.mean(yc * yc, axis=-1, keepdims=True)
        yn = yc * lax.rsqrt(var + GN_EPS) * gain
        g = g_ref[rows(n), :].astype(F32)
        o_ref[rows(n), :] = (g * jax.nn.sigmoid(g) * yn).astype(o_ref.dtype)
        return carry

    lax.fori_loop(0, n_chunks, outputs, 0, unroll=2)


def _retention(z_ret, log_decay, gn_gain, batch, seq_len, n_heads=N_HEADS):
    m = batch * seq_len
    n_chunks = seq_len // RET_CHUNK
    blk = (seq_len, HEAD_DIM)
    blocks = 5 * _nbytes(blk, BF16) + _nbytes((1, HEAD_DIM), F32)
    scratch = (2 * _nbytes((n_chunks, HEAD_DIM, HEAD_DIM), F32)
               + 2 * _nbytes((n_chunks, HEAD_DIM, HEAD_DIM), BF16))

    def col(c):
        return lambda b, h: (b, h + c * n_heads)

    return pl.pallas_call(
        functools.partial(_ret_kernel, seq_len=seq_len),
        grid=(batch, n_heads),
        in_specs=[pl.BlockSpec(memory_space=pltpu.SMEM),
                  pl.BlockSpec(blk, col(0)), pl.BlockSpec(blk, col(1)),
                  pl.BlockSpec(blk, col(2)), pl.BlockSpec(blk, col(3)),
                  pl.BlockSpec((1, HEAD_DIM), lambda b, h: (0, h))],
        out_specs=pl.BlockSpec(blk, lambda b, h: (b, h)),
        out_shape=jax.ShapeDtypeStruct((m, n_heads * HEAD_DIM), BF16),
        scratch_shapes=[pltpu.VMEM((n_chunks, HEAD_DIM, HEAD_DIM), F32),
                        pltpu.VMEM((n_chunks, HEAD_DIM, HEAD_DIM), F32),
                        pltpu.VMEM((n_chunks, HEAD_DIM, HEAD_DIM), BF16),
                        pltpu.VMEM((n_chunks, HEAD_DIM, HEAD_DIM), BF16)],
        compiler_params=_params(blocks, scratch),
        name="retention",
    )(log_decay, z_ret, z_ret, z_ret, z_ret, gn_gain.reshape(1, -1))


def _attn_kernel(slope_ref, q_ref, k_ref, v_ref, qg_ref, kg_ref, o_ref,
                 qs, ks, qd, kd, vd, acc_s, mx_s, den_s, *, seq_len):
    bq = ATT_BLOCK
    h = pl.program_id(1)
    slope = slope_ref[h]

    q = q_ref[...]
    qs[...] = (q * lax.rsqrt(jnp.mean(q * q, axis=-1, keepdims=True) + RMS_EPS)
               * qg_ref[...]) * (HEAD_DIM ** -0.5)
    k = k_ref[...]
    ks[...] = k * lax.rsqrt(jnp.mean(k * k, axis=-1, keepdims=True) + RMS_EPS) * kg_ref[...]

    for bi, d in enumerate(BRANCH_DILATIONS):
        m = seq_len // d
        nb = m // bq
        kw = min(bq + 2 * BRANCH_HALF, m)
        log_nb = nb.bit_length() - 1
        assert nb >= 1 and (1 << log_nb) == nb

        if d == 1:
            qd[...] = qs[...].astype(BF16)
            kd[...] = ks[...].astype(BF16)
            vd[...] = v_ref[...].astype(BF16)
        else:
            for r in range(d):
                dst = pl.ds(r * m, m)
                src = pl.ds(r, m, stride=d)
                qd[dst, :] = qs[src, :].astype(BF16)
                kd[dst, :] = ks[src, :].astype(BF16)
                vd[dst, :] = v_ref[src, :].astype(BF16)

        bias_step = slope * float(d)
        acc_b, mx_b, den_b = acc_s.at[bi], mx_s.at[bi], den_s.at[bi]

        def block(t, carry, d=d, m=m, nb=nb, kw=kw, log_nb=log_nb, bias_step=bias_step,
                  acc_b=acc_b, mx_b=mx_b, den_b=den_b):
            tl = t & (nb - 1)
            q0 = tl * bq
            k0 = jnp.clip(q0 - BRANCH_HALF, 0, m - kw)
            base = (t - tl) * bq
            qb = qd[pl.ds(pl.multiple_of(t * bq, bq), bq), :]
            kb = kd[pl.ds(pl.multiple_of(base + k0, BRANCH_HALF), kw), :]
            vb = vd[pl.ds(pl.multiple_of(base + k0, BRANCH_HALF), kw), :]
            s = lax.dot_general(qb, kb, (((1,), (1,)), ((), ())), preferred_element_type=F32)
            row = lax.broadcasted_iota(jnp.int32, (bq, kw), 0)
            col = lax.broadcasted_iota(jnp.int32, (bq, kw), 1)
            dist = jnp.abs(col - row + (k0 - q0))
            s = jnp.where(dist <= BRANCH_HALF, s - bias_step * dist.astype(F32), NEG_BIG)
            mx = jnp.max(s, axis=-1, keepdims=True)
            e = jnp.exp(s - mx)
            den = jnp.sum(e, axis=-1, keepdims=True)
            acc = jnp.dot(e.astype(BF16), vb, preferred_element_type=F32)
            if d == 1:
                dst = pl.ds(pl.multiple_of(t * bq, bq), bq)
            else:
                dst = pl.ds((t >> log_nb) + d * q0, bq, stride=d)
            acc_b[dst, :] = acc
            mx_b[dst, :] = jnp.broadcast_to(mx, (bq, HEAD_DIM))
            den_b[dst, :] = jnp.broadcast_to(den, (bq, HEAD_DIM))
            return carry

        lax.fori_loop(0, seq_len // bq, block, 0, unroll=2)

    def merge(t, carry):
        rows = pl.ds(pl.multiple_of(t * bq, bq), bq)
        m0, m1, m2 = mx_s[0, rows, :], mx_s[1, rows, :], mx_s[2, rows, :]
        top = jnp.maximum(jnp.maximum(m0, m1), m2)
        w0, w1, w2 = jnp.exp(m0 - top), jnp.exp(m1 - top), jnp.exp(m2 - top)
        num = w0 * acc_s[0, rows, :] + w1 * acc_s[1, rows, :] + w2 * acc_s[2, rows, :]
        den = w0 * den_s[0, rows, :] + w1 * den_s[1, rows, :] + w2 * den_s[2, rows, :]
        o_ref[rows, :] = (num / den).astype(o_ref.dtype)
        return carry

    lax.fori_loop(0, seq_len // bq, merge, 0, unroll=2)


def _alibi_slopes(n):
    return jnp.asarray(2.0 ** (-8.0 * np.arange(1, n + 1) / n), F32)


def _attention(z_att, q_gain, k_gain, batch, seq_len, n_heads=N_HEADS):
    m = batch * seq_len
    blk = (seq_len, HEAD_DIM)
    blocks = 3 * _nbytes(blk, F32) + _nbytes(blk, BF16) + 2 * _nbytes((1, HEAD_DIM), F32)
    scratch = (2 * _nbytes(blk, F32) + 3 * _nbytes(blk, BF16)
               + 3 * _nbytes((3,) + blk, F32))

    def col(c):
        return lambda b, h: (b, h + c * n_heads)

    return pl.pallas_call(
        functools.partial(_attn_kernel, seq_len=seq_len),
        grid=(batch, n_heads),
        in_specs=[pl.BlockSpec(memory_space=pltpu.SMEM),
                  pl.BlockSpec(blk, col(0)), pl.BlockSpec(blk, col(1)), pl.BlockSpec(blk, col(2)),
                  pl.BlockSpec((1, HEAD_DIM), lambda b, h: (0, 0)),
                  pl.BlockSpec((1, HEAD_DIM), lambda b, h: (0, 0))],
        out_specs=pl.BlockSpec(blk, lambda b, h: (b, h)),
        out_shape=jax.ShapeDtypeStruct((m, n_heads * HEAD_DIM), BF16),
        scratch_shapes=[pltpu.VMEM(blk, F32), pltpu.VMEM(blk, F32),
                        pltpu.VMEM(blk, BF16), pltpu.VMEM(blk, BF16), pltpu.VMEM(blk, BF16),
                        pltpu.VMEM((3,) + blk, F32), pltpu.VMEM((3,) + blk, F32),
                        pltpu.VMEM((3,) + blk, F32)],
        compiler_params=_params(blocks, scratch),
        name="dilated_attention",
    )(_alibi_slopes(n_heads), z_att, z_att, z_att,
      q_gain.reshape(1, -1), k_gain.reshape(1, -1))


def _residual_outputs(x_new, x_ref, xb_ref, ssq_ref):
    x_ref[...] = x_new
    xb_ref[...] = x_new.astype(BF16)
    ssq_ref[...] = _lane_partial_ssq(x_new)


def _mm_out_kernel(ret_ref, att_ref, wt_ref, wb_ref, x_ref, xo_ref, xb_ref, ssq_ref):
    acc = jnp.dot(ret_ref[...], wt_ref[...], preferred_element_type=F32)
    acc = acc + jnp.dot(att_ref[...], wb_ref[...], preferred_element_type=F32)
    _residual_outputs(x_ref[...] + acc, xo_ref, xb_ref, ssq_ref)


def _mm_out(ret, att, w, x, tm=1024, tn=512):
    m = x.shape[0]
    kh = MIX_HALF
    nj = D_MODEL // tn
    blocks = (2 * _nbytes((tm, kh), BF16) + 2 * _nbytes((kh, tn), BF16) + 2 * _nbytes((tm, tn), F32)
              + _nbytes((tm, tn), BF16) + _nbytes((tm, LANES), F32))
    return pl.pallas_call(
        _mm_out_kernel,
        grid=(m // tm, nj),
        in_specs=[pl.BlockSpec((tm, kh), lambda i, j: (i, 0)),
                  pl.BlockSpec((tm, kh), lambda i, j: (i, 0)),
                  pl.BlockSpec((kh, tn), lambda i, j: (0, j)),
                  pl.BlockSpec((kh, tn), lambda i, j: (1, j)),
                  pl.BlockSpec((tm, tn), lambda i, j: (i, j))],
        out_specs=[pl.BlockSpec((tm, tn), lambda i, j: (i, j)),
                   pl.BlockSpec((tm, tn), lambda i, j: (i, j)),
                   pl.BlockSpec((tm, LANES), lambda i, j: (i, j))],
        out_shape=[jax.ShapeDtypeStruct((m, D_MODEL), F32),
                   jax.ShapeDtypeStruct((m, D_MODEL), BF16),
                   jax.ShapeDtypeStruct((m, nj * LANES), F32)],
        compiler_params=_params(blocks, 2 * _nbytes((tm, tn), F32)),
        name="mm_out",
    )(ret, att, w, w, x)


def _mm_gateup_kernel(xb_ref, ssq_ref, wg_ref, wu_ref, h_ref):
    r = _row_scale(ssq_ref)
    xb = xb_ref[...]
    a = jnp.dot(xb, wg_ref[...], preferred_element_type=F32) * r
    u = jnp.dot(xb, wu_ref[...], preferred_element_type=F32) * r
    h_ref[...] = (a * jax.nn.sigmoid(a) * u).astype(h_ref.dtype)


def _mm_gateup(xb, ssq, wg, wu, tm=1024, tn=512):
    m, k = xb.shape
    s = ssq.shape[1]
    n = wg.shape[1]
    blocks = (_nbytes((tm, k), BF16) + _nbytes((tm, s), F32) + 2 * _nbytes((k, tn), BF16)
              + _nbytes((tm, tn), BF16))
    return pl.pallas_call(
        _mm_gateup_kernel,
        grid=(m // tm, n // tn),
        in_specs=[pl.BlockSpec((tm, k), lambda i, j: (i, 0)),
                  pl.BlockSpec((tm, s), lambda i, j: (i, 0)),
                  pl.BlockSpec((k, tn), lambda i, j: (0, j)),
                  pl.BlockSpec((k, tn), lambda i, j: (0, j))],
        out_specs=pl.BlockSpec((tm, tn), lambda i, j: (i, j)),
        out_shape=jax.ShapeDtypeStruct((m, n), BF16),
        compiler_params=_params(blocks, 3 * _nbytes((tm, tn), F32)),
        name="mm_gateup",
    )(xb, ssq, wg, wu)


def _mm_down_kernel(h_ref, w_ref, x_ref, xo_ref, xb_ref, ssq_ref):
    acc = jnp.dot(h_ref[...], w_ref[...], preferred_element_type=F32)
    _residual_outputs(x_ref[...] + acc, xo_ref, xb_ref, ssq_ref)


def _mm_down(hid, w, x, tm=512, tn=512):
    m, k = hid.shape
    nj = D_MODEL // tn
    blocks = (_nbytes((tm, k), BF16) + _nbytes((k, tn), BF16) + 2 * _nbytes((tm, tn), F32)
              + _nbytes((tm, tn), BF16) + _nbytes((tm, LANES), F32))
    return pl.pallas_call(
        _mm_down_kernel,
        grid=(m // tm, nj),
        in_specs=[pl.BlockSpec((tm, k), lambda i, j: (i, 0)),
                  pl.BlockSpec((k, tn), lambda i, j: (0, j)),
                  pl.BlockSpec((tm, tn), lambda i, j: (i, j))],
        out_specs=[pl.BlockSpec((tm, tn), lambda i, j: (i, j)),
                   pl.BlockSpec((tm, tn), lambda i, j: (i, j)),
                   pl.BlockSpec((tm, LANES), lambda i, j: (i, j))],
        out_shape=[jax.ShapeDtypeStruct((m, D_MODEL), F32),
                   jax.ShapeDtypeStruct((m, D_MODEL), BF16),
                   jax.ShapeDtypeStruct((m, nj * LANES), F32)],
        compiler_params=_params(blocks, _nbytes((tm, tn), F32)),
        name="mm_down",
    )(hid, w, x)


def _mm_ple_kernel(xb_ref, ssq_ref, wg_ref, p_ref, wp_ref, x_ref, xo_ref, xb_out_ref, ssq_out_ref):
    e = jnp.dot(xb_ref[...], wg_ref[...], preferred_element_type=F32) * _row_scale(ssq_ref)
    proj = jnp.dot(p_ref[...].astype(BF16), wp_ref[...], preferred_element_type=F32)
    _residual_outputs(x_ref[...] + jax.nn.sigmoid(e) * proj, xo_ref, xb_out_ref, ssq_out_ref)


def _mm_ple(xb, ssq, wg, p, wp, x, tm=1024, tn=512):
    m, k = xb.shape
    s = ssq.shape[1]
    nj = D_MODEL // tn
    blocks = (_nbytes((tm, k), BF16) + _nbytes((tm, s), F32) + _nbytes((k, tn), BF16)
              + _nbytes((tm, PLE_DIM), F32) + _nbytes((PLE_DIM, tn), BF16)
              + 2 * _nbytes((tm, tn), F32) + _nbytes((tm, tn), BF16) + _nbytes((tm, LANES), F32))
    return pl.pallas_call(
        _mm_ple_kernel,
        grid=(m // tm, nj),
        in_specs=[pl.BlockSpec((tm, k), lambda i, j: (i, 0)),
                  pl.BlockSpec((tm, s), lambda i, j: (i, 0)),
                  pl.BlockSpec((k, tn), lambda i, j: (0, j)),
                  pl.BlockSpec((tm, PLE_DIM), lambda i, j: (i, 0)),
                  pl.BlockSpec((PLE_DIM, tn), lambda i, j: (0, j)),
                  pl.BlockSpec((tm, tn), lambda i, j: (i, j))],
        out_specs=[pl.BlockSpec((tm, tn), lambda i, j: (i, j)),
                   pl.BlockSpec((tm, tn), lambda i, j: (i, j)),
                   pl.BlockSpec((tm, LANES), lambda i, j: (i, j))],
        out_shape=[jax.ShapeDtypeStruct((m, D_MODEL), F32),
                   jax.ShapeDtypeStruct((m, D_MODEL), BF16),
                   jax.ShapeDtypeStruct((m, nj * LANES), F32)],
        compiler_params=_params(blocks, 2 * _nbytes((tm, tn), F32)),
        name="mm_ple",
    )(xb, ssq, wg, p, wp, x)


def _layer_weights(i, ln_mix, w_in, w_out, ln_ffn, w_gate, w_up, w_down, ln_ple, w_ple_gate,
                   w_ple_proj):
    pad = D_FF_PAD - D_FF
    return dict(
        w_in=(ln_mix[i][:, None] * w_in[i]).astype(BF16),
        w_out=w_out[i].astype(BF16),
        w_gate=jnp.pad((ln_ffn[i][:, None] * w_gate[i]).astype(BF16), ((0, 0), (0, pad))),
        w_up=jnp.pad((ln_ffn[i][:, None] * w_up[i]).astype(BF16), ((0, 0), (0, pad))),
        w_down=jnp.pad(w_down[i].astype(BF16), ((0, pad), (0, 0))),
        w_ple_gate=(ln_ple[i][:, None] * w_ple_gate[i]).astype(BF16),
        w_ple_proj=w_ple_proj[i].astype(BF16),
    )


def _run_trunk(x, p, layers, ret_log_decay, ret_gn, q_norm, k_norm):
    batch, seq_len, _ = x.shape
    m = batch * seq_len
    x = x.reshape(m, D_MODEL)
    xb, ssq = _prep(x)
    for i, lw in enumerate(layers):
        z_ret = _mm_in(xb, ssq, lw["w_in"], 0, 4 * MIX_HALF, BF16)
        z_att = _mm_in(xb, ssq, lw["w_in"], 4 * MIX_HALF, 3 * MIX_HALF, F32)
        ret = _retention(z_ret, ret_log_decay[i], ret_gn[i], batch, seq_len)
        att = _attention(z_att, q_norm[i], k_norm[i], batch, seq_len)
        x, xb, ssq = _mm_out(ret, att, lw["w_out"], x)
        hid = _mm_gateup(xb, ssq, lw["w_gate"], lw["w_up"])
        x, xb, ssq = _mm_down(hid, lw["w_down"], x)
        x, xb, ssq = _mm_ple(xb, ssq, lw["w_ple_gate"], p[i].reshape(m, PLE_DIM),
                             lw["w_ple_proj"], x)
    return x.reshape(batch, seq_len, D_MODEL)


def kernel(x_prompt, x_sample, p_prompt, p_sample, ln_mix, w_in, ret_log_decay, ret_gn, q_norm,
           k_norm, w_out, ln_ffn, w_gate, w_up, w_down, ln_ple, w_ple_gate, w_ple_proj):
    depth = w_in.shape[0]
    layers = [_layer_weights(i, ln_mix, w_in, w_out, ln_ffn, w_gate, w_up, w_down, ln_ple,
                             w_ple_gate, w_ple_proj) for i in range(depth)]
    y_prompt = _run_trunk(x_prompt, p_prompt, layers, ret_log_decay, ret_gn, q_norm, k_norm)
    y_sample = _run_trunk(x_sample, p_sample, layers, ret_log_decay, ret_gn, q_norm, k_norm)
    return (y_prompt, y_sample)
```

```python
import functools

import jax
import jax.numpy as jnp
import numpy as np
from jax import lax
from jax.experimental import pallas as pl
from jax.experimental.pallas import tpu as pltpu

D_MODEL = 4096
HEAD_DIM = 128
N_HEADS = 16
MIX_HALF = N_HEADS * HEAD_DIM
D_FF = 11008
PLE_DIM = 256
RMS_EPS = 1e-6
GN_EPS = 1e-5
NEG_BIG = -1e30
RET_CHUNK = 128
RET_GROUP = 8
BRANCH_DILATIONS = (1, 4, 16)
BRANCH_HALF = 64
LANES = 128
ATT_BLOCK = 128
ATT_GROUP = 8

F32 = jnp.float32
BF16 = jnp.bfloat16
VMEM_HEADROOM = 6 * 1024 * 1024


def _nbytes(shape, dtype):
    return int(np.prod(shape)) * jnp.dtype(dtype).itemsize


def _params(block_bytes, scratch_bytes=0, n_grid=2):
    limit = 2 * block_bytes + scratch_bytes + VMEM_HEADROOM
    return pltpu.CompilerParams(
        dimension_semantics=("arbitrary",) * n_grid,
        vmem_limit_bytes=int(limit),
    )


def _row_scale(ssq_ref):
    tot = jnp.sum(ssq_ref[...], axis=-1, keepdims=True)
    return lax.rsqrt(tot * (1.0 / D_MODEL) + RMS_EPS)


def _lane_partial_ssq(x):
    sq = x * x
    acc = sq[:, 0:LANES]
    for c in range(1, x.shape[1] // LANES):
        acc = acc + sq[:, c * LANES:(c + 1) * LANES]
    return acc


def _prep_kernel(x_ref, xb_ref, ssq_ref):
    x = x_ref[...]
    xb_ref[...] = x.astype(BF16)
    ssq_ref[...] = _lane_partial_ssq(x)


def _prep(x):
    m = x.shape[0]
    tm = 512
    blocks = _nbytes((tm, D_MODEL), F32) + _nbytes((tm, D_MODEL), BF16) + _nbytes((tm, LANES), F32)
    return pl.pallas_call(
        _prep_kernel,
        grid=(m // tm,),
        in_specs=[pl.BlockSpec((tm, D_MODEL), lambda i: (i, 0))],
        out_specs=[pl.BlockSpec((tm, D_MODEL), lambda i: (i, 0)),
                   pl.BlockSpec((tm, LANES), lambda i: (i, 0))],
        out_shape=[jax.ShapeDtypeStruct((m, D_MODEL), BF16),
                   jax.ShapeDtypeStruct((m, LANES), F32)],
        compiler_params=_params(blocks, n_grid=1),
        name="prep",
    )(x)


def _mm_in_kernel(xb_ref, ssq_ref, w_ref, o_ref):
    acc = jnp.dot(xb_ref[...], w_ref[...], preferred_element_type=F32)
    o_ref[...] = (acc * _row_scale(ssq_ref)).astype(o_ref.dtype)


def _mm_in(xb, ssq, w, col0, ncols, out_dtype, tm=1024, tn=512):
    m, k = xb.shape
    s = ssq.shape[1]
    jb0 = col0 // tn
    blocks = (_nbytes((tm, k), BF16) + _nbytes((tm, s), F32) + _nbytes((k, tn), BF16)
              + _nbytes((tm, tn), out_dtype))
    return pl.pallas_call(
        _mm_in_kernel,
        grid=(m // tm, ncols // tn),
        in_specs=[pl.BlockSpec((tm, k), lambda i, j: (i, 0)),
                  pl.BlockSpec((tm, s), lambda i, j: (i, 0)),
                  pl.BlockSpec((k, tn), lambda i, j: (0, j + jb0))],
        out_specs=pl.BlockSpec((tm, tn), lambda i, j: (i, j)),
        out_shape=jax.ShapeDtypeStruct((m, ncols), out_dtype),
        compiler_params=_params(blocks, _nbytes((tm, tn), F32)),
        name="mm_in",
    )(xb, ssq, w)


def _ret_kernel(ld_ref, q_ref, k_ref, v_ref, g_ref, gn_ref, o_ref,
                kvf_s, kvb_s, sf_s, sb_s, *, seq_len):
    c = RET_CHUNK
    n_chunks = seq_len // c
    h = pl.program_id(1)
    lf = ld_ref[0, h]
    lb = ld_ref[1, h]
    scale = HEAD_DIM ** -0.5

    ii = lax.broadcasted_iota(jnp.int32, (c, c), 0)
    jj = lax.broadcasted_iota(jnp.int32, (c, c), 1)
    dist = (ii - jj).astype(F32)
    dmat = jnp.where(ii >= jj,
                     jnp.exp(lf * jnp.maximum(dist, 0.0)),
                     jnp.exp(lb * jnp.maximum(-dist, 0.0))) * scale
    pos = lax.broadcasted_iota(jnp.int32, (c, 1), 0).astype(F32)
    zeta_f = jnp.exp(lf * (c - 1.0 - pos)) * scale
    zeta_b = jnp.exp(lb * pos) * scale
    xi_f = jnp.exp(lf * (pos + 1.0))
    xi_b = jnp.exp(lb * (c - pos))
    one = jnp.ones((1, 1), F32)
    cd_f = jnp.exp(one * (lf * c))
    cd_b = jnp.exp(one * (lb * c))

    tdims = (((0,), (0,)), ((), ()))

    def rows(n):
        return pl.ds(pl.multiple_of(n * c, c), c)

    def summaries(n, carry):
        k = k_ref[rows(n), :]
        v = v_ref[rows(n), :].astype(F32)
        vz = jnp.concatenate([(v * zeta_f).astype(BF16), (v * zeta_b).astype(BF16)], axis=1)
        kv = lax.dot_general(k, vz, tdims, preferred_element_type=F32)
        kvf_s[n] = kv[:, :HEAD_DIM]
        kvb_s[n] = kv[:, HEAD_DIM:]
        return carry

    lax.fori_loop(0, n_chunks, summaries, 0, unroll=8)

    def scan_f(n, st):
        sf_s[n] = st.astype(BF16)
        return st * cd_f + kvf_s[n]

    lax.fori_loop(0, n_chunks, scan_f, jnp.zeros((HEAD_DIM, HEAD_DIM), F32))

    def scan_b(t, st):
        n = n_chunks - 1 - t
        sb_s[n] = st.astype(BF16)
        return st * cd_b + kvb_s[n]

    lax.fori_loop(0, n_chunks, scan_b, jnp.zeros((HEAD_DIM, HEAD_DIM), F32))

    gain = gn_ref[...]

    def outputs(ng, carry):
        scores, crosses = [], []
        for j in range(RET_GROUP):
            n = ng * RET_GROUP + j
            q = q_ref[rows(n), :]
            k = k_ref[rows(n), :]
            scores.append(lax.dot_general(q, k, (((1,), (1,)), ((), ())),
                                          preferred_element_type=F32))
            st = jnp.concatenate([sf_s[n], sb_s[n]], axis=1)
            crosses.append(jnp.dot(q, st, preferred_element_type=F32))
        probs = [(s * dmat).astype(BF16) for s in scores]
        for j in range(RET_GROUP):
            n = ng * RET_GROUP + j
            intra = jnp.dot(probs[j], v_ref[rows(n), :], preferred_element_type=F32)
            cross = crosses[j]
            y = intra + cross[:, :HEAD_DIM] * xi_f + cross[:, HEAD_DIM:] * xi_b
            mu = jnp.mean(y, axis=-1, keepdims=True)
            yc = y - mu
            var = jnp.mean(yc * yc, axis=-1, keepdims=True)
            yn = yc * lax.rsqrt(var + GN_EPS) * gain
            g = g_ref[rows(n), :].astype(F32)
            o_ref[rows(n), :] = (g * jax.nn.sigmoid(g) * yn).astype(o_ref.dtype)
        return carry

    lax.fori_loop(0, n_chunks // RET_GROUP, outputs, 0)


def _retention(z_ret, log_decay, gn_gain, batch, seq_len, n_heads=N_HEADS):
    m = batch * seq_len
    n_chunks = seq_len // RET_CHUNK
    assert seq_len % (RET_CHUNK * RET_GROUP) == 0
    blk = (seq_len, HEAD_DIM)
    blocks = 5 * _nbytes(blk, BF16) + _nbytes((1, HEAD_DIM), F32)
    scratch = (2 * _nbytes((n_chunks, HEAD_DIM, HEAD_DIM), F32)
               + 2 * _nbytes((n_chunks, HEAD_DIM, HEAD_DIM), BF16))

    def col(c):
        return lambda b, h: (b, h + c * n_heads)

    return pl.pallas_call(
        functools.partial(_ret_kernel, seq_len=seq_len),
        grid=(batch, n_heads),
        in_specs=[pl.BlockSpec(memory_space=pltpu.SMEM),
                  pl.BlockSpec(blk, col(0)), pl.BlockSpec(blk, col(1)),
                  pl.BlockSpec(blk, col(2)), pl.BlockSpec(blk, col(3)),
                  pl.BlockSpec((1, HEAD_DIM), lambda b, h: (0, h))],
        out_specs=pl.BlockSpec(blk, lambda b, h: (b, h)),
        out_shape=jax.ShapeDtypeStruct((m, n_heads * HEAD_DIM), BF16),
        scratch_shapes=[pltpu.VMEM((n_chunks, HEAD_DIM, HEAD_DIM), F32),
                        pltpu.VMEM((n_chunks, HEAD_DIM, HEAD_DIM), F32),
                        pltpu.VMEM((n_chunks, HEAD_DIM, HEAD_DIM), BF16),
                        pltpu.VMEM((n_chunks, HEAD_DIM, HEAD_DIM), BF16)],
        compiler_params=_params(blocks, scratch),
        name="retention",
    )(log_decay, z_ret, z_ret, z_ret, z_ret, gn_gain.reshape(1, -1))


def _attn_kernel(slope_ref, q_ref, k_ref, v_ref, qg_ref, kg_ref, o_ref,
                 qs, ks, q4, k4, v4, qd, kd, vd, bias_s, out_s, lse_s, *, seq_len):
    bq = ATT_BLOCK
    kw_max = bq + 2 * BRANCH_HALF
    h = pl.program_id(0)

    @pl.when(pl.program_id(1) == 0)
    def _():
        slope = slope_ref[h]
        row = lax.broadcasted_iota(jnp.int32, (bq, kw_max), 0)
        col = lax.broadcasted_iota(jnp.int32, (bq, kw_max), 1)
        for bi, d in enumerate(BRANCH_DILATIONS):
            kw = min(kw_max, seq_len // d)
            for var, off in enumerate((0, -BRANCH_HALF, bq - kw)):
                dist = jnp.abs(col - row + off)
                bias_s[bi, var] = jnp.where(dist <= BRANCH_HALF,
                                            dist.astype(F32) * (-slope * float(d)), NEG_BIG)

    q = q_ref[...]
    qs[...] = (q * lax.rsqrt(jnp.mean(q * q, axis=-1, keepdims=True) + RMS_EPS)
               * qg_ref[...]) * (HEAD_DIM ** -0.5)
    k = k_ref[...]
    ks[...] = k * lax.rsqrt(jnp.mean(k * k, axis=-1, keepdims=True) + RMS_EPS) * kg_ref[...]

    m4 = seq_len // 4
    m16 = seq_len // 16
    for bi, d in enumerate(BRANCH_DILATIONS):
        m = seq_len // d
        nb = m // bq
        kw = min(kw_max, m)
        log_nb = nb.bit_length() - 1
        assert nb >= 1 and (1 << log_nb) == nb

        if d == 1:
            qd[...] = qs[...].astype(BF16)
            kd[...] = ks[...].astype(BF16)
            vd[...] = v_ref[...].astype(BF16)
        elif d == 4:
            for r in range(4):
                dst = pl.ds(r * m4, m4)
                src = pl.ds(r, m4, stride=4)
                for src_ref, f32_ref, bf_ref in ((qs, q4, qd), (ks, k4, kd), (v_ref, v4, vd)):
                    val = src_ref[src, :]
                    f32_ref[dst, :] = val
                    bf_ref[dst, :] = val.astype(BF16)
        else:
            for r4 in range(4):
                for r1 in range(4):
                    dst = pl.ds((r4 + 4 * r1) * m16, m16)
                    src = pl.ds(r4 * m4 + r1, m16, stride=4)
                    qd[dst, :] = q4[src, :].astype(BF16)
                    kd[dst, :] = k4[src, :].astype(BF16)
                    vd[dst, :] = v4[src, :].astype(BF16)

        out_b, lse_b, bias_b = out_s.at[bi], lse_s.at[bi], bias_s.at[bi]

        def group(tg, carry, d=d, m=m, nb=nb, kw=kw, log_nb=log_nb,
                  out_b=out_b, lse_b=lse_b, bias_b=bias_b):
            geo, scores, probs = [], [], []
            for g in range(ATT_GROUP):
                t = tg * ATT_GROUP + g
                tl = t & (nb - 1)
                q0 = tl * bq
                k0 = jnp.clip(q0 - BRANCH_HALF, 0, m - kw)
                kstart = pl.multiple_of((t - tl) * bq + k0, BRANCH_HALF)
                var = jnp.where(tl == 0, 0, jnp.where(tl == nb - 1, 2, 1))
                geo.append((t, q0, kstart, var))
                qb = qd[pl.ds(pl.multiple_of(t * bq, bq), bq), :]
                kb = kd[pl.ds(kstart, kw), :]
                scores.append(lax.dot_general(qb, kb, (((1,), (1,)), ((), ())),
                                              preferred_element_type=F32))
            for g in range(ATT_GROUP):
                s = scores[g] + bias_b[geo[g][3], :, 0:kw]
                mx = jnp.max(s, axis=-1, keepdims=True)
                e = jnp.exp(s - mx)
                den = jnp.sum(e, axis=-1, keepdims=True)
                probs.append((e.astype(BF16), mx, den))
            for g in range(ATT_GROUP):
                t, q0, kstart, _ = geo[g]
                e, mx, den = probs[g]
                acc = jnp.dot(e, vd[pl.ds(kstart, kw), :], preferred_element_type=F32)
                if d == 1:
                    dst = pl.ds(pl.multiple_of(t * bq, bq), bq)
                else:
                    dst = pl.ds((t >> log_nb) + d * q0, bq, stride=d)
                out_b[dst, :] = acc / den
                lse_b[dst, :] = jnp.broadcast_to(mx + jnp.log(den), (bq, HEAD_DIM))
            return carry

        lax.fori_loop(0, seq_len // (bq * ATT_GROUP), group, 0)

    def merge(t, carry):
        rows = pl.ds(pl.multiple_of(t * bq, bq), bq)
        l0, l1, l2 = lse_s[0, rows, :], lse_s[1, rows, :], lse_s[2, rows, :]
        top = jnp.maximum(jnp.maximum(l0, l1), l2)
        w0, w1, w2 = jnp.exp(l0 - top), jnp.exp(l1 - top), jnp.exp(l2 - top)
        num = w0 * out_s[0, rows, :] + w1 * out_s[1, rows, :] + w2 * out_s[2, rows, :]
        o_ref[rows, :] = (num / (w0 + w1 + w2)).astype(o_ref.dtype)
        return carry

    lax.fori_loop(0, seq_len // bq, merge, 0, unroll=4)


def _alibi_slopes(n):
    return jnp.asarray(2.0 ** (-8.0 * np.arange(1, n + 1) / n), F32)


def _attention(z_att, q_gain, k_gain, batch, seq_len, n_heads=N_HEADS):
    m = batch * seq_len
    blk = (seq_len, HEAD_DIM)
    bias_shape = (len(BRANCH_DILATIONS), 3, ATT_BLOCK, ATT_BLOCK + 2 * BRANCH_HALF)
    blocks = 3 * _nbytes(blk, F32) + _nbytes(blk, BF16) + 2 * _nbytes((1, HEAD_DIM), F32)
    scratch = (5 * _nbytes(blk, F32) + 3 * _nbytes(blk, BF16) + _nbytes(bias_shape, F32)
               + 2 * _nbytes((3,) + blk, F32))

    def col(c):
        return lambda h, b: (b, h + c * n_heads)

    return pl.pallas_call(
        functools.partial(_attn_kernel, seq_len=seq_len),
        grid=(n_heads, batch),
        in_specs=[pl.BlockSpec(memory_space=pltpu.SMEM),
                  pl.BlockSpec(blk, col(0)), pl.BlockSpec(blk, col(1)), pl.BlockSpec(blk, col(2)),
                  pl.BlockSpec((1, HEAD_DIM), lambda h, b: (0, 0)),
                  pl.BlockSpec((1, HEAD_DIM), lambda h, b: (0, 0))],
        out_specs=pl.BlockSpec(blk, lambda h, b: (b, h)),
        out_shape=jax.ShapeDtypeStruct((m, n_heads * HEAD_DIM), BF16),
        scratch_shapes=[pltpu.VMEM(blk, F32)] * 5
                       + [pltpu.VMEM(blk, BF16)] * 3
                       + [pltpu.VMEM(bias_shape, F32),
                          pltpu.VMEM((3,) + blk, F32), pltpu.VMEM((3,) + blk, F32)],
        compiler_params=_params(blocks, scratch),
        name="dilated_attention",
    )(_alibi_slopes(n_heads), z_att, z_att, z_att,
      q_gain.reshape(1, -1), k_gain.reshape(1, -1))


def _residual_outputs(x_new, x_ref, xb_ref, ssq_ref):
    x_ref[...] = x_new
    xb_ref[...] = x_new.astype(BF16)
    ssq_ref[...] = _lane_partial_ssq(x_new)


def _mm_out_kernel(ret_ref, att_ref, wt_ref, wb_ref, x_ref, xo_ref, xb_ref, ssq_ref):
    acc = jnp.dot(ret_ref[...], wt_ref[...], preferred_element_type=F32)
    acc = acc + jnp.dot(att_ref[...], wb_ref[...], preferred_element_type=F32)
    _residual_outputs(x_ref[...] + acc, xo_ref, xb_ref, ssq_ref)


def _mm_out(ret, att, w, x, tm=1024, tn=512):
    m = x.shape[0]
    kh = MIX_HALF
    nj = D_MODEL // tn
    blocks = (2 * _nbytes((tm, kh), BF16) + 2 * _nbytes((kh, tn), BF16) + 2 * _nbytes((tm, tn), F32)
              + _nbytes((tm, tn), BF16) + _nbytes((tm, LANES), F32))
    return pl.pallas_call(
        _mm_out_kernel,
        grid=(m // tm, nj),
        in_specs=[pl.BlockSpec((tm, kh), lambda i, j: (i, 0)),
                  pl.BlockSpec((tm, kh), lambda i, j: (i, 0)),
                  pl.BlockSpec((kh, tn), lambda i, j: (0, j)),
                  pl.BlockSpec((kh, tn), lambda i, j: (1, j)),
                  pl.BlockSpec((tm, tn), lambda i, j: (i, j))],
        out_specs=[pl.BlockSpec((tm, tn), lambda i, j: (i, j)),
                   pl.BlockSpec((tm, tn), lambda i, j: (i, j)),
                   pl.BlockSpec((tm, LANES), lambda i, j: (i, j))],
        out_shape=[jax.ShapeDtypeStruct((m, D_MODEL), F32),
                   jax.ShapeDtypeStruct((m, D_MODEL), BF16),
                   jax.ShapeDtypeStruct((m, nj * LANES), F32)],
        compiler_params=_params(blocks, 2 * _nbytes((tm, tn), F32)),
        name="mm_out",
    )(ret, att, w, w, x)


def _mm_gateup_kernel(xb_ref, ssq_ref, wg_ref, wu_ref, h_ref):
    r = _row_scale(ssq_ref)
    xb = xb_ref[...]
    a = jnp.dot(xb, wg_ref[...], preferred_element_type=F32) * r
    u = jnp.dot(xb, wu_ref[...], preferred_element_type=F32) * r
    h_ref[...] = (a * jax.nn.sigmoid(a) * u).astype(h_ref.dtype)


def _mm_gateup(xb, ssq, wg, wu, tm=1024, tn=512):
    m, k = xb.shape
    s = ssq.shape[1]
    n = wg.shape[1]
    blocks = (_nbytes((tm, k), BF16) + _nbytes((tm, s), F32) + 2 * _nbytes((k, tn), BF16)
              + _nbytes((tm, tn), BF16))
    return pl.pallas_call(
        _mm_gateup_kernel,
        grid=(m // tm, pl.cdiv(n, tn)),
        in_specs=[pl.BlockSpec((tm, k), lambda i, j: (i, 0)),
                  pl.BlockSpec((tm, s), lambda i, j: (i, 0)),
                  pl.BlockSpec((k, tn), lambda i, j: (0, j)),
                  pl.BlockSpec((k, tn), lambda i, j: (0, j))],
        out_specs=pl.BlockSpec((tm, tn), lambda i, j: (i, j)),
        out_shape=jax.ShapeDtypeStruct((m, n), BF16),
        compiler_params=_params(blocks, 3 * _nbytes((tm, tn), F32)),
        name="mm_gateup",
    )(xb, ssq, wg, wu)


def _mm_down_kernel(h_ref, w_ref, x_ref, xo_ref, xb_ref, ssq_ref):
    acc = jnp.dot(h_ref[...], w_ref[...], preferred_element_type=F32)
    _residual_outputs(x_ref[...] + acc, xo_ref, xb_ref, ssq_ref)


def _mm_down(hid, w, x, tm=512, tn=512):
    m, k = hid.shape
    nj = D_MODEL // tn
    blocks = (_nbytes((tm, k), BF16) + _nbytes((k, tn), BF16) + 2 * _nbytes((tm, tn), F32)
              + _nbytes((tm, tn), BF16) + _nbytes((tm, LANES), F32))
    return pl.pallas_call(
        _mm_down_kernel,
        grid=(m // tm, nj),
        in_specs=[pl.BlockSpec((tm, k), lambda i, j: (i, 0)),
                  pl.BlockSpec((k, tn), lambda i, j: (0, j)),
                  pl.BlockSpec((tm, tn), lambda i, j: (i, j))],
        out_specs=[pl.BlockSpec((tm, tn), lambda i, j: (i, j)),
                   pl.BlockSpec((tm, tn), lambda i, j: (i, j)),
                   pl.BlockSpec((tm, LANES), lambda i, j: (i, j))],
        out_shape=[jax.ShapeDtypeStruct((m, D_MODEL), F32),
                   jax.ShapeDtypeStruct((m, D_MODEL), BF16),
                   jax.ShapeDtypeStruct((m, nj * LANES), F32)],
        compiler_params=_params(blocks, _nbytes((tm, tn), F32)),
        name="mm_down",
    )(hid, w, x)


def _mm_ple_kernel(xb_ref, ssq_ref, wg_ref, p_ref, wp_ref, x_ref, xo_ref, xb_out_ref, ssq_out_ref):
    e = jnp.dot(xb_ref[...], wg_ref[...], preferred_element_type=F32) * _row_scale(ssq_ref)
    proj = jnp.dot(p_ref[...].astype(BF16), wp_ref[...], preferred_element_type=F32)
    _residual_outputs(x_ref[...] + jax.nn.sigmoid(e) * proj, xo_ref, xb_out_ref, ssq_out_ref)


def _mm_ple(xb, ssq, wg, p, wp, x, tm=1024, tn=512):
    m, k = xb.shape
    s = ssq.shape[1]
    nj = D_MODEL // tn
    blocks = (_nbytes((tm, k), BF16) + _nbytes((tm, s), F32) + _nbytes((k, tn), BF16)
              + _nbytes((tm, PLE_DIM), F32) + _nbytes((PLE_DIM, tn), BF16)
              + 2 * _nbytes((tm, tn), F32) + _nbytes((tm, tn), BF16) + _nbytes((tm, LANES), F32))
    return pl.pallas_call(
        _mm_ple_kernel,
        grid=(m // tm, nj),
        in_specs=[pl.BlockSpec((tm, k), lambda i, j: (i, 0)),
                  pl.BlockSpec((tm, s), lambda i, j: (i, 0)),
                  pl.BlockSpec((k, tn), lambda i, j: (0, j)),
                  pl.BlockSpec((tm, PLE_DIM), lambda i, j: (i, 0)),
                  pl.BlockSpec((PLE_DIM, tn), lambda i, j: (0, j)),
                  pl.BlockSpec((tm, tn), lambda i, j: (i, j))],
        out_specs=[pl.BlockSpec((tm, tn), lambda i, j: (i, j)),
                   pl.BlockSpec((tm, tn), lambda i, j: (i, j)),
                   pl.BlockSpec((tm, LANES), lambda i, j: (i, j))],
        out_shape=[jax.ShapeDtypeStruct((m, D_MODEL), F32),
                   jax.ShapeDtypeStruct((m, D_MODEL), BF16),
                   jax.ShapeDtypeStruct((m, nj * LANES), F32)],
        compiler_params=_params(blocks, 2 * _nbytes((tm, tn), F32)),
        name="mm_ple",
    )(xb, ssq, wg, p, wp, x)


def _layer_weights(i, ln_mix, w_in, w_out, ln_ffn, w_gate, w_up, w_down, ln_ple, w_ple_gate,
                   w_ple_proj):
    return dict(
        w_in=(ln_mix[i][:, None] * w_in[i]).astype(BF16),
        w_out=w_out[i].astype(BF16),
        w_gate=(ln_ffn[i][:, None] * w_gate[i]).astype(BF16),
        w_up=(ln_ffn[i][:, None] * w_up[i]).astype(BF16),
        w_down=w_down[i].astype(BF16),
        w_ple_gate=(ln_ple[i][:, None] * w_ple_gate[i]).astype(BF16),
        w_ple_proj=w_ple_proj[i].astype(BF16),
    )


def _run_trunk(x, p, layers, ret_log_decay, ret_gn, q_norm, k_norm):
    batch, seq_len, _ = x.shape
    m = batch * seq_len
    x = x.reshape(m, D_MODEL)
    xb, ssq = _prep(x)
    for i, lw in enumerate(layers):
        z_ret = _mm_in(xb, ssq, lw["w_in"], 0, 4 * MIX_HALF, BF16)
        z_att = _mm_in(xb, ssq, lw["w_in"], 4 * MIX_HALF, 3 * MIX_HALF, F32)
        ret = _retention(z_ret, ret_log_decay[i], ret_gn[i], batch, seq_len)
        att = _attention(z_att, q_norm[i], k_norm[i], batch, seq_len)
        x, xb, ssq = _mm_out(ret, att, lw["w_out"], x)
        hid = _mm_gateup(xb, ssq, lw["w_gate"], lw["w_up"])
        x, xb, ssq = _mm_down(hid, lw["w_down"], x)
        x, xb, ssq = _mm_ple(xb, ssq, lw["w_ple_gate"], p[i].reshape(m, PLE_DIM),
                             lw["w_ple_proj"], x)
    return x.reshape(batch, seq_len, D_MODEL)


def kernel(x_prompt, x_sample, p_prompt, p_sample, ln_mix, w_in, ret_log_decay, ret_gn, q_norm,
           k_norm, w_out, ln_ffn, w_gate, w_up, w_down, ln_ple, w_ple_gate, w_ple_proj):
    depth = w_in.shape[0]
    layers = [_layer_weights(i, ln_mix, w_in, w_out, ln_ffn, w_gate, w_up, w_down, ln_ple,
                             w_ple_gate, w_ple_proj) for i in range(depth)]
    y_prompt = _run_trunk(x_prompt, p_prompt, layers, ret_log_decay, ret_gn, q_norm, k_norm)
    y_sample = _run_trunk(x_sample, p_sample, layers, ret_log_decay, ret_gn, q_norm, k_norm)
    return (y_prompt, y_sample)
```

```python
import functools

import jax
import jax.numpy as jnp
import numpy as np
from jax import lax
from jax.experimental import pallas as pl
from jax.experimental.pallas import tpu as pltpu

D_MODEL = 4096
HEAD_DIM = 128
N_HEADS = 16
MIX_HALF = N_HEADS * HEAD_DIM
PLE_DIM = 256
RMS_EPS = 1e-6
GN_EPS = 1e-5
NEG_BIG = -1e30
RET_CHUNK = 128
RET_GROUP = 8
BRANCH_DILATIONS = (1, 4, 16)
BRANCH_HALF = 64
LANES = 128
ATT_BLOCK = 128
ATT_GROUP = 8

F32 = jnp.float32
BF16 = jnp.bfloat16
VMEM_HEADROOM = 5 * 1024 * 1024


def _nbytes(shape, dtype):
    return int(np.prod(shape)) * jnp.dtype(dtype).itemsize


def _params(block_bytes, scratch_bytes=0, n_grid=2):
    limit = 2 * block_bytes + scratch_bytes + VMEM_HEADROOM
    return pltpu.CompilerParams(
        dimension_semantics=("arbitrary",) * n_grid,
        vmem_limit_bytes=int(limit),
    )


def _row_scale(ssq_ref):
    tot = jnp.sum(ssq_ref[...], axis=-1, keepdims=True)
    return lax.rsqrt(tot * (1.0 / D_MODEL) + RMS_EPS)


def _lane_partial_ssq(x):
    sq = x * x
    acc = sq[:, 0:LANES]
    for c in range(1, x.shape[1] // LANES):
        acc = acc + sq[:, c * LANES:(c + 1) * LANES]
    return acc


def _mm_call(kernel_fn, *, grid, in_specs, out_specs, out_shape, operands, block_bytes,
             temp_bytes, name, jobs=()):
    n_in, n_out = len(in_specs), len(out_specs)
    gj = grid[1]
    steps = grid[0] * grid[1]
    in_specs, out_specs, out_shape = list(in_specs), list(out_specs), list(out_shape)
    operands = list(operands)
    job_in_specs, job_operands, job_out_specs, job_out_shape, has_gain = [], [], [], [], []
    for src, layer, gain, rb in jobs:
        _, r, c = src.shape
        nbk = r // rb
        assert r % rb == 0 and nbk <= steps, (name, r, rb, steps)

        def blk(i, j, nbk=nbk):
            return jnp.minimum(i * gj + j, nbk - 1)

        job_in_specs.append(pl.BlockSpec((None, rb, c),
                                         lambda i, j, blk=blk, layer=layer: (layer, blk(i, j), 0)))
        job_operands.append(src)
        if gain is not None:
            job_in_specs.append(pl.BlockSpec((None, rb, 1),
                                             lambda i, j, blk=blk, layer=layer: (layer, blk(i, j), 0)))
            job_operands.append(gain)
        has_gain.append(gain is not None)
        job_out_specs.append(pl.BlockSpec((rb, c), lambda i, j, blk=blk: (blk(i, j), 0)))
        job_out_shape.append(jax.ShapeDtypeStruct((r, c), BF16))
        block_bytes += _nbytes((rb, c), F32) + _nbytes((rb, c), BF16) + _nbytes((rb, LANES), F32)

    def body(*refs):
        main_in = refs[:n_in]
        pos = n_in
        job_in = []
        for hg in has_gain:
            job_in.append((refs[pos], refs[pos + 1] if hg else None))
            pos += 2 if hg else 1
        main_out = refs[pos:pos + n_out]
        job_out = refs[pos + n_out:]
        for (src_ref, gain_ref), dst_ref in zip(job_in, job_out):
            w = src_ref[...]
            if gain_ref is not None:
                w = w * gain_ref[...]
            dst_ref[...] = w.astype(BF16)
        kernel_fn(*main_in, *main_out)

    res = pl.pallas_call(
        body,
        grid=grid,
        in_specs=in_specs + job_in_specs,
        out_specs=out_specs + job_out_specs,
        out_shape=out_shape + job_out_shape,
        compiler_params=_params(block_bytes, temp_bytes),
        name=name,
    )(*operands, *job_operands)
    return tuple(res[:n_out]), tuple(res[n_out:])


def _prep_kernel(x_ref, xb_ref, ssq_ref):
    x = x_ref[...]
    xb_ref[...] = x.astype(BF16)
    ssq_ref[...] = _lane_partial_ssq(x)


def _prep(x):
    m = x.shape[0]
    tm = 512
    blocks = _nbytes((tm, D_MODEL), F32) + _nbytes((tm, D_MODEL), BF16) + _nbytes((tm, LANES), F32)
    return pl.pallas_call(
        _prep_kernel,
        grid=(m // tm,),
        in_specs=[pl.BlockSpec((tm, D_MODEL), lambda i: (i, 0))],
        out_specs=[pl.BlockSpec((tm, D_MODEL), lambda i: (i, 0)),
                   pl.BlockSpec((tm, LANES), lambda i: (i, 0))],
        out_shape=[jax.ShapeDtypeStruct((m, D_MODEL), BF16),
                   jax.ShapeDtypeStruct((m, LANES), F32)],
        compiler_params=_params(blocks, n_grid=1),
        name="prep",
    )(x)


def _mm_in_kernel(xb_ref, ssq_ref, w_ref, o_ref):
    acc = jnp.dot(xb_ref[...], w_ref[...], preferred_element_type=F32)
    o_ref[...] = (acc * _row_scale(ssq_ref)).astype(o_ref.dtype)


def _mm_in(xb, ssq, w, col0, ncols, out_dtype, name, tm=1024, tn=512, jobs=()):
    m, k = xb.shape
    s = ssq.shape[1]
    jb0 = col0 // tn
    blocks = (_nbytes((tm, k), BF16) + _nbytes((tm, s), F32) + _nbytes((k, tn), BF16)
              + _nbytes((tm, tn), out_dtype))
    (z,), cast = _mm_call(
        _mm_in_kernel,
        grid=(m // tm, ncols // tn),
        in_specs=[pl.BlockSpec((tm, k), lambda i, j: (i, 0)),
                  pl.BlockSpec((tm, s), lambda i, j: (i, 0)),
                  pl.BlockSpec((k, tn), lambda i, j: (0, j + jb0))],
        out_specs=[pl.BlockSpec((tm, tn), lambda i, j: (i, j))],
        out_shape=[jax.ShapeDtypeStruct((m, ncols), out_dtype)],
        operands=[xb, ssq, w],
        block_bytes=blocks, temp_bytes=_nbytes((tm, tn), F32), name=name, jobs=jobs)
    return z, cast


def _ret_kernel(ld_ref, q_ref, k_ref, v_ref, g_ref, gn_ref, o_ref,
                kvf_s, kvb_s, sf_s, sb_s, *, seq_len):
    c = RET_CHUNK
    n_chunks = seq_len // c
    h = pl.program_id(1)
    lf = ld_ref[0, h]
    lb = ld_ref[1, h]
    scale = HEAD_DIM ** -0.5

    ii = lax.broadcasted_iota(jnp.int32, (c, c), 0)
    jj = lax.broadcasted_iota(jnp.int32, (c, c), 1)
    dist = (ii - jj).astype(F32)
    dmat = jnp.where(ii >= jj,
                     jnp.exp(lf * jnp.maximum(dist, 0.0)),
                     jnp.exp(lb * jnp.maximum(-dist, 0.0))) * scale
    pos = lax.broadcasted_iota(jnp.int32, (c, 1), 0).astype(F32)
    zeta_f = jnp.exp(lf * (c - 1.0 - pos)) * scale
    zeta_b = jnp.exp(lb * pos) * scale
    xi_f = jnp.exp(lf * (pos + 1.0))
    xi_b = jnp.exp(lb * (c - pos))
    one = jnp.ones((1, 1), F32)
    cd_f = jnp.exp(one * (lf * c))
    cd_b = jnp.exp(one * (lb * c))

    tdims = (((0,), (0,)), ((), ()))

    def rows(n):
        return pl.ds(pl.multiple_of(n * c, c), c)

    def summaries(n, carry):
        k = k_ref[rows(n), :]
        v = v_ref[rows(n), :].astype(F32)
        vz = jnp.concatenate([(v * zeta_f).astype(BF16), (v * zeta_b).astype(BF16)], axis=1)
        kv = lax.dot_general(k, vz, tdims, preferred_element_type=F32)
        kvf_s[n] = kv[:, :HEAD_DIM]
        kvb_s[n] = kv[:, HEAD_DIM:]
        return carry

    lax.fori_loop(0, n_chunks, summaries, 0, unroll=8)

    def scan_f(n, st):
        sf_s[n] = st.astype(BF16)
        return st * cd_f + kvf_s[n]

    lax.fori_loop(0, n_chunks, scan_f, jnp.zeros((HEAD_DIM, HEAD_DIM), F32))

    def scan_b(t, st):
        n = n_chunks - 1 - t
        sb_s[n] = st.astype(BF16)
        return st * cd_b + kvb_s[n]

    lax.fori_loop(0, n_chunks, scan_b, jnp.zeros((HEAD_DIM, HEAD_DIM), F32))

    gain = gn_ref[...]

    def outputs(ng, carry):
        scores, crosses = [], []
        for j in range(RET_GROUP):
            n = ng * RET_GROUP + j
            q = q_ref[rows(n), :]
            k = k_ref[rows(n), :]
            scores.append(lax.dot_general(q, k, (((1,), (1,)), ((), ())),
                                          preferred_element_type=F32))
            st = jnp.concatenate([sf_s[n], sb_s[n]], axis=1)
            crosses.append(jnp.dot(q, st, preferred_element_type=F32))
        probs = [(s * dmat).astype(BF16) for s in scores]
        for j in range(RET_GROUP):
            n = ng * RET_GROUP + j
            intra = jnp.dot(probs[j], v_ref[rows(n), :], preferred_element_type=F32)
            cross = crosses[j]
            y = intra + cross[:, :HEAD_DIM] * xi_f + cross[:, HEAD_DIM:] * xi_b
            mu = jnp.mean(y, axis=-1, keepdims=True)
            yc = y - mu
            var = jnp.mean(yc * yc, axis=-1, keepdims=True)
            yn = yc * lax.rsqrt(var + GN_EPS) * gain
            g = g_ref[rows(n), :].astype(F32)
            o_ref[rows(n), :] = (g * jax.nn.sigmoid(g) * yn).astype(o_ref.dtype)
        return carry

    lax.fori_loop(0, n_chunks // RET_GROUP, outputs, 0)


def _retention(z_ret, log_decay, gn_gain, batch, seq_len, n_heads=N_HEADS):
    m = batch * seq_len
    n_chunks = seq_len // RET_CHUNK
    assert seq_len % (RET_CHUNK * RET_GROUP) == 0
    blk = (seq_len, HEAD_DIM)
    blocks = 5 * _nbytes(blk, BF16) + _nbytes((1, HEAD_DIM), F32)
    scratch = (2 * _nbytes((n_chunks, HEAD_DIM, HEAD_DIM), F32)
               + 2 * _nbytes((n_chunks, HEAD_DIM, HEAD_DIM), BF16))

    def col(c):
        return lambda b, h: (b, h + c * n_heads)

    return pl.pallas_call(
        functools.partial(_ret_kernel, seq_len=seq_len),
        grid=(batch, n_heads),
        in_specs=[pl.BlockSpec(memory_space=pltpu.SMEM),
                  pl.BlockSpec(blk, col(0)), pl.BlockSpec(blk, col(1)),
                  pl.BlockSpec(blk, col(2)), pl.BlockSpec(blk, col(3)),
                  pl.BlockSpec((1, HEAD_DIM), lambda b, h: (0, h))],
        out_specs=pl.BlockSpec(blk, lambda b, h: (b, h)),
        out_shape=jax.ShapeDtypeStruct((m, n_heads * HEAD_DIM), BF16),
        scratch_shapes=[pltpu.VMEM((n_chunks, HEAD_DIM, HEAD_DIM), F32),
                        pltpu.VMEM((n_chunks, HEAD_DIM, HEAD_DIM), F32),
                        pltpu.VMEM((n_chunks, HEAD_DIM, HEAD_DIM), BF16),
                        pltpu.VMEM((n_chunks, HEAD_DIM, HEAD_DIM), BF16)],
        compiler_params=_params(blocks, scratch),
        name="retention",
    )(log_decay, z_ret, z_ret, z_ret, z_ret, gn_gain.reshape(1, -1))


def _attn_kernel(slope_ref, qraw_ref, kraw_ref, v_ref, qg_ref, kg_ref, o_ref,
                 q_ref, k_ref, q4, k4, v4, qd, kd, vd, bias_s, out_s, lse_s, *, seq_len):
    bq = ATT_BLOCK
    kw_max = bq + 2 * BRANCH_HALF
    h = pl.program_id(0)

    @pl.when(pl.program_id(1) == 0)
    def _():
        slope = slope_ref[h]
        row = lax.broadcasted_iota(jnp.int32, (bq, kw_max), 0)
        col = lax.broadcasted_iota(jnp.int32, (bq, kw_max), 1)
        for bi, d in enumerate(BRANCH_DILATIONS):
            kw = min(kw_max, seq_len // d)
            for var, off in enumerate((0, -BRANCH_HALF, bq - kw)):
                dist = jnp.abs(col - row + off)
                bias_s[bi, var] = jnp.where(dist <= BRANCH_HALF,
                                            dist.astype(F32) * (-slope * float(d)), NEG_BIG)

    q = qraw_ref[...]
    q_ref[...] = (q * lax.rsqrt(jnp.mean(q * q, axis=-1, keepdims=True) + RMS_EPS)
                  * qg_ref[...]) * (HEAD_DIM ** -0.5)
    k = kraw_ref[...]
    k_ref[...] = k * lax.rsqrt(jnp.mean(k * k, axis=-1, keepdims=True) + RMS_EPS) * kg_ref[...]

    m4 = seq_len // 4
    m16 = seq_len // 16
    for d in (4, 16, 1):
        bi = BRANCH_DILATIONS.index(d)
        m = seq_len // d
        nb = m // bq
        kw = min(kw_max, m)
        log_nb = nb.bit_length() - 1
        assert nb >= 1 and (1 << log_nb) == nb

        if d == 4:
            for r in range(4):
                dst = pl.ds(r * m4, m4)
                src = pl.ds(r, m4, stride=4)
                for src_ref, f32_ref, bf_ref in ((q_ref, q4, qd), (k_ref, k4, kd), (v_ref, v4, vd)):
                    val = src_ref[src, :]
                    f32_ref[dst, :] = val
                    bf_ref[dst, :] = val.astype(BF16)
        elif d == 16:
            for r4 in range(4):
                for r1 in range(4):
                    dst = pl.ds((r4 + 4 * r1) * m16, m16)
                    src = pl.ds(r4 * m4 + r1, m16, stride=4)
                    qd[dst, :] = q4[src, :].astype(BF16)
                    kd[dst, :] = k4[src, :].astype(BF16)
                    vd[dst, :] = v4[src, :].astype(BF16)
        else:
            qd[...] = q_ref[...].astype(BF16)
            kd[...] = k_ref[...].astype(BF16)
            vd[...] = v_ref[...].astype(BF16)

        bias_b = bias_s.at[bi]

        def group(tg, carry, d=d, m=m, nb=nb, kw=kw, log_nb=log_nb, bias_b=bias_b):
            geo, scores, probs = [], [], []
            for g in range(ATT_GROUP):
                t = tg * ATT_GROUP + g
                tl = t & (nb - 1)
                q0 = tl * bq
                k0 = jnp.clip(q0 - BRANCH_HALF, 0, m - kw)
                kstart = pl.multiple_of((t - tl) * bq + k0, BRANCH_HALF)
                var = jnp.where(tl == 0, 0, jnp.where(tl == nb - 1, 2, 1))
                geo.append((t, q0, kstart, var))
                qb = qd[pl.ds(pl.multiple_of(t * bq, bq), bq), :]
                kb = kd[pl.ds(kstart, kw), :]
                scores.append(lax.dot_general(qb, kb, (((1,), (1,)), ((), ())),
                                              preferred_element_type=F32))
            for g in range(ATT_GROUP):
                s = scores[g] + bias_b[geo[g][3], :, 0:kw]
                mx = jnp.max(s, axis=-1, keepdims=True)
                e = jnp.exp(s - mx)
                den = jnp.sum(e, axis=-1, keepdims=True)
                probs.append((e.astype(BF16), mx, den))
            for g in range(ATT_GROUP):
                t, q0, kstart, _ = geo[g]
                e, mx, den = probs[g]
                acc = jnp.dot(e, vd[pl.ds(kstart, kw), :], preferred_element_type=F32)
                if d == 1:
                    rows = pl.ds(pl.multiple_of(t * bq, bq), bq)
                    l4, l16 = lse_s[0, rows, :], lse_s[1, rows, :]
                    shift = jnp.maximum(jnp.maximum(l4, l16), mx)
                    w1 = jnp.exp(mx - shift)
                    w4 = jnp.exp(l4 - shift)
                    w16 = jnp.exp(l16 - shift)
                    num = w1 * acc + w4 * out_s[0, rows, :] + w16 * out_s[1, rows, :]
                    o_ref[rows, :] = (num / (w1 * den + w4 + w16)).astype(o_ref.dtype)
                else:
                    slot = 0 if d == 4 else 1
                    dst = pl.ds((t >> log_nb) + d * q0, bq, stride=d)
                    out_s[slot, dst, :] = acc / den
                    lse_s[slot, dst, :] = jnp.broadcast_to(mx + jnp.log(den), (bq, HEAD_DIM))
            return carry

        lax.fori_loop(0, seq_len // (bq * ATT_GROUP), group, 0)


def _alibi_slopes(n):
    return jnp.asarray(2.0 ** (-8.0 * np.arange(1, n + 1) / n), F32)


def _attention(z_att, q_gain, k_gain, batch, seq_len, n_heads=N_HEADS):
    m = batch * seq_len
    assert (seq_len // ATT_BLOCK) % ATT_GROUP == 0
    assert seq_len // max(BRANCH_DILATIONS) >= ATT_BLOCK
    blk = (seq_len, HEAD_DIM)
    bias_shape = (len(BRANCH_DILATIONS), 3, ATT_BLOCK, ATT_BLOCK + 2 * BRANCH_HALF)
    blocks = 3 * _nbytes(blk, F32) + _nbytes(blk, BF16) + 2 * _nbytes((1, HEAD_DIM), F32)
    scratch = (5 * _nbytes(blk, F32) + 3 * _nbytes(blk, BF16) + _nbytes(bias_shape, F32)
               + 2 * _nbytes((2,) + blk, F32))

    def col(c):
        return lambda h, b: (b, h + c * n_heads)

    return pl.pallas_call(
        functools.partial(_attn_kernel, seq_len=seq_len),
        grid=(n_heads, batch),
        in_specs=[pl.BlockSpec(memory_space=pltpu.SMEM),
                  pl.BlockSpec(blk, col(0)), pl.BlockSpec(blk, col(1)), pl.BlockSpec(blk, col(2)),
                  pl.BlockSpec((1, HEAD_DIM), lambda h, b: (0, 0)),
                  pl.BlockSpec((1, HEAD_DIM), lambda h, b: (0, 0))],
        out_specs=pl.BlockSpec(blk, lambda h, b: (b, h)),
        out_shape=jax.ShapeDtypeStruct((m, n_heads * HEAD_DIM), BF16),
        scratch_shapes=[pltpu.VMEM(blk, F32)] * 5
                       + [pltpu.VMEM(blk, BF16)] * 3
                       + [pltpu.VMEM(bias_shape, F32),
                          pltpu.VMEM((2,) + blk, F32), pltpu.VMEM((2,) + blk, F32)],
        compiler_params=_params(blocks, scratch),
        name="dilated_attention",
    )(_alibi_slopes(n_heads), z_att, z_att, z_att,
      q_gain.reshape(1, -1), k_gain.reshape(1, -1))


def _residual_outputs(x_new, x_ref, xb_ref, ssq_ref):
    x_ref[...] = x_new
    xb_ref[...] = x_new.astype(BF16)
    part = _lane_partial_ssq(x_new)
    j = pl.program_id(1)

    @pl.when(j == 0)
    def _():
        ssq_ref[...] = part

    @pl.when(j > 0)
    def _():
        ssq_ref[...] += part


def _residual_specs(m, tm, tn):
    specs = [pl.BlockSpec((tm, tn), lambda i, j: (i, j)),
             pl.BlockSpec((tm, tn), lambda i, j: (i, j)),
             pl.BlockSpec((tm, LANES), lambda i, j: (i, 0))]
    shapes = [jax.ShapeDtypeStruct((m, D_MODEL), F32),
              jax.ShapeDtypeStruct((m, D_MODEL), BF16),
              jax.ShapeDtypeStruct((m, LANES), F32)]
    nbytes = _nbytes((tm, tn), F32) + _nbytes((tm, tn), BF16) + _nbytes((tm, LANES), F32)
    return specs, shapes, nbytes


def _mm_out_kernel(ret_ref, att_ref, wt_ref, wb_ref, x_ref, xo_ref, xb_ref, ssq_ref):
    acc = jnp.dot(ret_ref[...], wt_ref[...], preferred_element_type=F32)
    acc = acc + jnp.dot(att_ref[...], wb_ref[...], preferred_element_type=F32)
    _residual_outputs(x_ref[...] + acc, xo_ref, xb_ref, ssq_ref)


def _mm_out(ret, att, w, x, tm=1024, tn=512, jobs=()):
    m = x.shape[0]
    kh = MIX_HALF
    out_specs, out_shape, out_bytes = _residual_specs(m, tm, tn)
    blocks = (2 * _nbytes((tm, kh), BF16) + 2 * _nbytes((kh, tn), BF16) + _nbytes((tm, tn), F32)
              + out_bytes)
    return _mm_call(
        _mm_out_kernel,
        grid=(m // tm, D_MODEL // tn),
        in_specs=[pl.BlockSpec((tm, kh), lambda i, j: (i, 0)),
                  pl.BlockSpec((tm, kh), lambda i, j: (i, 0)),
                  pl.BlockSpec((kh, tn), lambda i, j: (0, j)),
                  pl.BlockSpec((kh, tn), lambda i, j: (1, j)),
                  pl.BlockSpec((tm, tn), lambda i, j: (i, j))],
        out_specs=out_specs, out_shape=out_shape,
        operands=[ret, att, w, w, x],
        block_bytes=blocks, temp_bytes=2 * _nbytes((tm, tn), F32), name="mm_out", jobs=jobs)


def _mm_gateup_kernel(xb_ref, ssq_ref, wg_ref, wu_ref, h_ref):
    r = _row_scale(ssq_ref)
    xb = xb_ref[...]
    a = jnp.dot(xb, wg_ref[...], preferred_element_type=F32) * r
    u = jnp.dot(xb, wu_ref[...], preferred_element_type=F32) * r
    h_ref[...] = (a * jax.nn.sigmoid(a) * u).astype(h_ref.dtype)


def _mm_gateup(xb, ssq, wg, wu, tm=1024, tn=512, jobs=()):
    m, k = xb.shape
    s = ssq.shape[1]
    n = wg.shape[1]
    blocks = (_nbytes((tm, k), BF16) + _nbytes((tm, s), F32) + 2 * _nbytes((k, tn), BF16)
              + _nbytes((tm, tn), BF16))
    (hid,), cast = _mm_call(
        _mm_gateup_kernel,
        grid=(m // tm, pl.cdiv(n, tn)),
        in_specs=[pl.BlockSpec((tm, k), lambda i, j: (i, 0)),
                  pl.BlockSpec((tm, s), lambda i, j: (i, 0)),
                  pl.BlockSpec((k, tn), lambda i, j: (0, j)),
                  pl.BlockSpec((k, tn), lambda i, j: (0, j))],
        out_specs=[pl.BlockSpec((tm, tn), lambda i, j: (i, j))],
        out_shape=[jax.ShapeDtypeStruct((m, n), BF16)],
        operands=[xb, ssq, wg, wu],
        block_bytes=blocks, temp_bytes=4 * _nbytes((tm, tn), F32), name="mm_gateup", jobs=jobs)
    return hid, cast


def _mm_down_kernel(h_ref, w_ref, x_ref, xo_ref, xb_ref, ssq_ref):
    acc = jnp.dot(h_ref[...], w_ref[...], preferred_element_type=F32)
    _residual_outputs(x_ref[...] + acc, xo_ref, xb_ref, ssq_ref)


def _mm_down(hid, w, x, tm=512, tn=512, jobs=()):
    m, k = hid.shape
    out_specs, out_shape, out_bytes = _residual_specs(m, tm, tn)
    blocks = _nbytes((tm, k), BF16) + _nbytes((k, tn), BF16) + _nbytes((tm, tn), F32) + out_bytes
    return _mm_call(
        _mm_down_kernel,
        grid=(m // tm, D_MODEL // tn),
        in_specs=[pl.BlockSpec((tm, k), lambda i, j: (i, 0)),
                  pl.BlockSpec((k, tn), lambda i, j: (0, j)),
                  pl.BlockSpec((tm, tn), lambda i, j: (i, j))],
        out_specs=out_specs, out_shape=out_shape,
        operands=[hid, w, x],
        block_bytes=blocks, temp_bytes=_nbytes((tm, tn), F32), name="mm_down", jobs=jobs)


def _mm_ple_kernel(xb_ref, ssq_ref, wg_ref, p_ref, wp_ref, x_ref, xo_ref, xb_out_ref, ssq_out_ref):
    e = jnp.dot(xb_ref[...], wg_ref[...], preferred_element_type=F32) * _row_scale(ssq_ref)
    proj = jnp.dot(p_ref[...].astype(BF16), wp_ref[...], preferred_element_type=F32)
    _residual_outputs(x_ref[...] + jax.nn.sigmoid(e) * proj, xo_ref, xb_out_ref, ssq_out_ref)


def _mm_ple(xb, ssq, wg, p, wp, x, tm=1024, tn=512, jobs=()):
    m, k = xb.shape
    s = ssq.shape[1]
    out_specs, out_shape, out_bytes = _residual_specs(m, tm, tn)
    blocks = (_nbytes((tm, k), BF16) + _nbytes((tm, s), F32) + _nbytes((k, tn), BF16)
              + _nbytes((tm, PLE_DIM), F32) + _nbytes((PLE_DIM, tn), BF16)
              + _nbytes((tm, tn), F32) + out_bytes)
    return _mm_call(
        _mm_ple_kernel,
        grid=(m // tm, D_MODEL // tn),
        in_specs=[pl.BlockSpec((tm, k), lambda i, j: (i, 0)),
                  pl.BlockSpec((tm, s), lambda i, j: (i, 0)),
                  pl.BlockSpec((k, tn), lambda i, j: (0, j)),
                  pl.BlockSpec((tm, PLE_DIM), lambda i, j: (i, 0)),
                  pl.BlockSpec((PLE_DIM, tn), lambda i, j: (0, j)),
                  pl.BlockSpec((tm, tn), lambda i, j: (i, j))],
        out_specs=out_specs, out_shape=out_shape,
        operands=[xb, ssq, wg, p, wp, x],
        block_bytes=blocks, temp_bytes=2 * _nbytes((tm, tn), F32), name="mm_ple", jobs=jobs)


_HOSTED = {
    (0, "mm_in_ret"): [("w_out", 0)],
    (0, "mm_in_att"): [("w_gate", 0)],
    (0, "mm_out"): [("w_up", 0)],
    (0, "mm_gateup"): [("w_down", 0), ("w_ple_gate", 0), ("w_in", 1)],
    (0, "mm_down"): [("w_out", 1)],
    (0, "mm_ple"): [("w_up", 1)],
    (1, "mm_in_ret"): [("w_gate", 1)],
    (1, "mm_gateup"): [("w_down", 1), ("w_ple_gate", 1)],
}
_GAIN_OF = {"w_in": "ln_mix", "w_gate": "ln_ffn", "w_up": "ln_ffn", "w_ple_gate": "ln_ple"}
_ROW_BLOCK = {"w_in": 32, "w_out": 32, "w_gate": 64, "w_up": 64, "w_down": 64, "w_ple_gate": 32}
_ROW_BLOCK_OVERRIDE = {(1, "mm_in_ret", "w_gate"): 32}


def _run_trunk(x, p, weights, raw, host):
    batch, seq_len, _ = x.shape
    m = batch * seq_len
    x = x.reshape(m, D_MODEL)
    xb, ssq = _prep(x)

    def jobs_for(layer, call):
        if not host:
            return [], []
        names = _HOSTED.get((layer, call), [])
        jobs = []
        for name, wl in names:
            gain = raw[_GAIN_OF[name]][..., None] if name in _GAIN_OF else None
            rb = _ROW_BLOCK_OVERRIDE.get((layer, call, name), _ROW_BLOCK[name])
            jobs.append((raw[name], wl, gain, rb))
        return names, jobs

    def store(names, cast):
        for (name, wl), w in zip(names, cast):
            weights[wl][name] = w

    for i in range(len(weights)):
        lw = weights[i]
        names, jobs = jobs_for(i, "mm_in_ret")
        z_ret, cast = _mm_in(xb, ssq, lw["w_in"], 0, 4 * MIX_HALF, BF16, "mm_in_ret", jobs=jobs)
        store(names, cast)
        names, jobs = jobs_for(i, "mm_in_att")
        z_att, cast = _mm_in(xb, ssq, lw["w_in"], 4 * MIX_HALF, 3 * MIX_HALF, F32, "mm_in_att",
                             jobs=jobs)
        store(names, cast)
        ret = _retention(z_ret, raw["ret_log_decay"][i], raw["ret_gn"][i], batch, seq_len)
        att = _attention(z_att, raw["q_norm"][i], raw["k_norm"][i], batch, seq_len)
        names, jobs = jobs_for(i, "mm_out")
        (x, xb, ssq), cast = _mm_out(ret, att, lw["w_out"], x, jobs=jobs)
        store(names, cast)
        names, jobs = jobs_for(i, "mm_gateup")
        hid, cast = _mm_gateup(xb, ssq, lw["w_gate"], lw["w_up"], jobs=jobs)
        store(names, cast)
        names, jobs = jobs_for(i, "mm_down")
        (x, xb, ssq), cast = _mm_down(hid, lw["w_down"], x, jobs=jobs)
        store(names, cast)
        names, jobs = jobs_for(i, "mm_ple")
        (x, xb, ssq), cast = _mm_ple(xb, ssq, lw["w_ple_gate"], p[i].reshape(m, PLE_DIM),
                                     lw["w_ple_proj"], x, jobs=jobs)
        store(names, cast)
    return x.reshape(batch, seq_len, D_MODEL)


def kernel(x_prompt, x_sample, p_prompt, p_sample, ln_mix, w_in, ret_log_decay, ret_gn, q_norm,
           k_norm, w_out, ln_ffn, w_gate, w_up, w_down, ln_ple, w_ple_gate, w_ple_proj):
    depth = w_in.shape[0]
    raw = dict(ln_mix=ln_mix, w_in=w_in, ret_log_decay=ret_log_decay, ret_gn=ret_gn, q_norm=q_norm,
               k_norm=k_norm, w_out=w_out, ln_ffn=ln_ffn, w_gate=w_gate, w_up=w_up, w_down=w_down,
               ln_ple=ln_ple, w_ple_gate=w_ple_gate, w_ple_proj=w_ple_proj)
    hosted = {(name, wl) for names in _HOSTED.values() for name, wl in names} if depth == 2 else set()
    weights = []
    for i in range(depth):
        lw = {"w_ple_proj": w_ple_proj[i].astype(BF16)}
        for name in ("w_in", "w_out", "w_gate", "w_up", "w_down", "w_ple_gate"):
            if (name, i) in hosted:
                continue
            w = raw[name][i]
            if name in _GAIN_OF:
                w = raw[_GAIN_OF[name]][i][:, None] * w
            lw[name] = w.astype(BF16)
        weights.append(lw)
    y_prompt = _run_trunk(x_prompt, p_prompt, weights, raw, host=depth == 2)
    y_sample = _run_trunk(x_sample, p_sample, weights, raw, host=False)
    return (y_prompt, y_sample)
```

```python
import functools

import jax
import jax.numpy as jnp
import numpy as np
from jax import lax
from jax.experimental import pallas as pl
from jax.experimental.pallas import tpu as pltpu

D_MODEL = 4096
HEAD_DIM = 128
N_HEADS = 16
MIX_HALF = N_HEADS * HEAD_DIM
PLE_DIM = 256
RMS_EPS = 1e-6
GN_EPS = 1e-5
NEG_BIG = -1e30
RET_CHUNK = 128
RET_GROUP = 8
BRANCH_DILATIONS = (1, 4, 16)
BRANCH_HALF = 64
LANES = 128
ATT_BLOCK = 128
ATT_GROUP = 8
MM_ROW_SPLIT = 1

F32 = jnp.float32
BF16 = jnp.bfloat16
VMEM_HEADROOM = 5 * 1024 * 1024


def _nbytes(shape, dtype):
    return int(np.prod(shape)) * jnp.dtype(dtype).itemsize


def _params(block_bytes, scratch_bytes=0, n_grid=2):
    limit = 2 * block_bytes + scratch_bytes + VMEM_HEADROOM
    return pltpu.CompilerParams(
        dimension_semantics=("arbitrary",) * n_grid,
        vmem_limit_bytes=int(limit),
    )


def _row_chunks(tm):
    nr = tm // MM_ROW_SPLIT
    return [pl.ds(c * nr, nr) for c in range(MM_ROW_SPLIT)]


def _row_scale(ssq_ref, rows=slice(None)):
    tot = jnp.sum(ssq_ref[rows, :], axis=-1, keepdims=True)
    return lax.rsqrt(tot * (1.0 / D_MODEL) + RMS_EPS)


def _sigmoid(x):
    return 0.5 * jnp.tanh(0.5 * x) + 0.5


def _lane_partial_ssq(x):
    sq = x * x
    acc = sq[:, 0:LANES]
    for c in range(1, x.shape[1] // LANES):
        acc = acc + sq[:, c * LANES:(c + 1) * LANES]
    return acc


def _mm_call(kernel_fn, *, grid, in_specs, out_specs, out_shape, operands, block_bytes,
             temp_bytes, name, jobs=()):
    n_in, n_out = len(in_specs), len(out_specs)
    gj = grid[1]
    steps = grid[0] * grid[1]
    in_specs, out_specs, out_shape = list(in_specs), list(out_specs), list(out_shape)
    operands = list(operands)
    job_in_specs, job_operands, job_out_specs, job_out_shape, has_gain = [], [], [], [], []
    for src, layer, gain, rb in jobs:
        _, r, c = src.shape
        nbk = r // rb
        assert r % rb == 0 and nbk <= steps, (name, r, rb, steps)

        def blk(i, j, nbk=nbk):
            return jnp.minimum(i * gj + j, nbk - 1)

        job_in_specs.append(pl.BlockSpec((None, rb, c),
                                         lambda i, j, blk=blk, layer=layer: (layer, blk(i, j), 0)))
        job_operands.append(src)
        if gain is not None:
            job_in_specs.append(pl.BlockSpec((None, rb, 1),
                                             lambda i, j, blk=blk, layer=layer: (layer, blk(i, j), 0)))
            job_operands.append(gain)
        has_gain.append(gain is not None)
        job_out_specs.append(pl.BlockSpec((rb, c), lambda i, j, blk=blk: (blk(i, j), 0)))
        job_out_shape.append(jax.ShapeDtypeStruct((r, c), BF16))
        block_bytes += _nbytes((rb, c), F32) + _nbytes((rb, c), BF16) + _nbytes((rb, LANES), F32)

    def body(*refs):
        main_in = refs[:n_in]
        pos = n_in
        job_in = []
        for hg in has_gain:
            job_in.append((refs[pos], refs[pos + 1] if hg else None))
            pos += 2 if hg else 1
        main_out = refs[pos:pos + n_out]
        job_out = refs[pos + n_out:]
        for (src_ref, gain_ref), dst_ref in zip(job_in, job_out):
            w = src_ref[...]
            if gain_ref is not None:
                w = w * gain_ref[...]
            dst_ref[...] = w.astype(BF16)
        kernel_fn(*main_in, *main_out)

    res = pl.pallas_call(
        body,
        grid=grid,
        in_specs=in_specs + job_in_specs,
        out_specs=out_specs + job_out_specs,
        out_shape=out_shape + job_out_shape,
        compiler_params=_params(block_bytes, temp_bytes),
        name=name,
    )(*operands, *job_operands)
    return tuple(res[:n_out]), tuple(res[n_out:])


def _prep_kernel(x_ref, xb_ref, ssq_ref):
    x = x_ref[...]
    xb_ref[...] = x.astype(BF16)
    ssq_ref[...] = _lane_partial_ssq(x)


def _prep(x):
    m = x.shape[0]
    tm = 512
    blocks = _nbytes((tm, D_MODEL), F32) + _nbytes((tm, D_MODEL), BF16) + _nbytes((tm, LANES), F32)
    return pl.pallas_call(
        _prep_kernel,
        grid=(m // tm,),
        in_specs=[pl.BlockSpec((tm, D_MODEL), lambda i: (i, 0))],
        out_specs=[pl.BlockSpec((tm, D_MODEL), lambda i: (i, 0)),
                   pl.BlockSpec((tm, LANES), lambda i: (i, 0))],
        out_shape=[jax.ShapeDtypeStruct((m, D_MODEL), BF16),
                   jax.ShapeDtypeStruct((m, LANES), F32)],
        compiler_params=_params(blocks, n_grid=1),
        name="prep",
    )(x)


def _mm_in_kernel(xb_ref, ssq_ref, w_ref, o_ref):
    for rows in _row_chunks(xb_ref.shape[0]):
        acc = jnp.dot(xb_ref[rows, :], w_ref[...], preferred_element_type=F32)
        o_ref[rows, :] = (acc * _row_scale(ssq_ref, rows)).astype(o_ref.dtype)


def _mm_in(xb, ssq, w, col0, ncols, out_dtype, name, tm=1024, tn=512, jobs=()):
    m, k = xb.shape
    s = ssq.shape[1]
    jb0 = col0 // tn
    blocks = (_nbytes((tm, k), BF16) + _nbytes((tm, s), F32) + _nbytes((k, tn), BF16)
              + _nbytes((tm, tn), out_dtype))
    (z,), cast = _mm_call(
        _mm_in_kernel,
        grid=(m // tm, ncols // tn),
        in_specs=[pl.BlockSpec((tm, k), lambda i, j: (i, 0)),
                  pl.BlockSpec((tm, s), lambda i, j: (i, 0)),
                  pl.BlockSpec((k, tn), lambda i, j: (0, j + jb0))],
        out_specs=[pl.BlockSpec((tm, tn), lambda i, j: (i, j))],
        out_shape=[jax.ShapeDtypeStruct((m, ncols), out_dtype)],
        operands=[xb, ssq, w],
        block_bytes=blocks, temp_bytes=_nbytes((tm, tn), F32), name=name, jobs=jobs)
    return z, cast


def _ret_kernel(ld_ref, q_ref, k_ref, v_ref, g_ref, gn_ref, o_ref,
                kvf_s, kvb_s, sf_s, sb_s, *, seq_len):
    c = RET_CHUNK
    n_chunks = seq_len // c
    h = pl.program_id(1)
    lf = ld_ref[0, h]
    lb = ld_ref[1, h]
    scale = HEAD_DIM ** -0.5

    ii = lax.broadcasted_iota(jnp.int32, (c, c), 0)
    jj = lax.broadcasted_iota(jnp.int32, (c, c), 1)
    dist = (ii - jj).astype(F32)
    dmat = jnp.where(ii >= jj,
                     jnp.exp(lf * jnp.maximum(dist, 0.0)),
                     jnp.exp(lb * jnp.maximum(-dist, 0.0))) * scale
    pos = lax.broadcasted_iota(jnp.int32, (c, 1), 0).astype(F32)
    zeta_f = jnp.exp(lf * (c - 1.0 - pos)) * scale
    zeta_b = jnp.exp(lb * pos) * scale
    xi_f = jnp.exp(lf * (pos + 1.0))
    xi_b = jnp.exp(lb * (c - pos))
    one = jnp.ones((1, 1), F32)
    cd_f = jnp.exp(one * (lf * c))
    cd_b = jnp.exp(one * (lb * c))

    tdims = (((0,), (0,)), ((), ()))

    def rows(n):
        return pl.ds(pl.multiple_of(n * c, c), c)

    def summaries(n, carry):
        k = k_ref[rows(n), :]
        v = v_ref[rows(n), :].astype(F32)
        vz = jnp.concatenate([(v * zeta_f).astype(BF16), (v * zeta_b).astype(BF16)], axis=1)
        kv = lax.dot_general(k, vz, tdims, preferred_element_type=F32)
        kvf_s[n] = kv[:, :HEAD_DIM]
        kvb_s[n] = kv[:, HEAD_DIM:]
        return carry

    lax.fori_loop(0, n_chunks, summaries, 0, unroll=8)

    def scan_f(n, st):
        sf_s[n] = st.astype(BF16)
        return st * cd_f + kvf_s[n]

    lax.fori_loop(0, n_chunks, scan_f, jnp.zeros((HEAD_DIM, HEAD_DIM), F32))

    def scan_b(t, st):
        n = n_chunks - 1 - t
        sb_s[n] = st.astype(BF16)
        return st * cd_b + kvb_s[n]

    lax.fori_loop(0, n_chunks, scan_b, jnp.zeros((HEAD_DIM, HEAD_DIM), F32))

    gain = gn_ref[...]

    def outputs(ng, carry):
        scores, crosses = [], []
        for j in range(RET_GROUP):
            n = ng * RET_GROUP + j
            q = q_ref[rows(n), :]
            k = k_ref[rows(n), :]
            scores.append(lax.dot_general(q, k, (((1,), (1,)), ((), ())),
                                          preferred_element_type=F32))
            st = jnp.concatenate([sf_s[n], sb_s[n]], axis=1)
            crosses.append(jnp.dot(q, st, preferred_element_type=F32))
        probs = [(s * dmat).astype(BF16) for s in scores]
        for j in range(RET_GROUP):
            n = ng * RET_GROUP + j
            intra = jnp.dot(probs[j], v_ref[rows(n), :], preferred_element_type=F32)
            cross = crosses[j]
            y = intra + cross[:, :HEAD_DIM] * xi_f + cross[:, HEAD_DIM:] * xi_b
            mu = jnp.mean(y, axis=-1, keepdims=True)
            yc = y - mu
            var = jnp.mean(yc * yc, axis=-1, keepdims=True)
            yn = yc * lax.rsqrt(var + GN_EPS) * gain
            g = g_ref[rows(n), :].astype(F32)
            o_ref[rows(n), :] = (g * _sigmoid(g) * yn).astype(o_ref.dtype)
        return carry

    lax.fori_loop(0, n_chunks // RET_GROUP, outputs, 0)


def _retention(z_ret, log_decay, gn_gain, batch, seq_len, n_heads=N_HEADS):
    m = batch * seq_len
    n_chunks = seq_len // RET_CHUNK
    assert seq_len % (RET_CHUNK * RET_GROUP) == 0
    blk = (seq_len, HEAD_DIM)
    blocks = 5 * _nbytes(blk, BF16) + _nbytes((1, HEAD_DIM), F32)
    scratch = (2 * _nbytes((n_chunks, HEAD_DIM, HEAD_DIM), F32)
               + 2 * _nbytes((n_chunks, HEAD_DIM, HEAD_DIM), BF16))

    def col(c):
        return lambda b, h: (b, h + c * n_heads)

    return pl.pallas_call(
        functools.partial(_ret_kernel, seq_len=seq_len),
        grid=(batch, n_heads),
        in_specs=[pl.BlockSpec(memory_space=pltpu.SMEM),
                  pl.BlockSpec(blk, col(0)), pl.BlockSpec(blk, col(1)),
                  pl.BlockSpec(blk, col(2)), pl.BlockSpec(blk, col(3)),
                  pl.BlockSpec((1, HEAD_DIM), lambda b, h: (0, h))],
        out_specs=pl.BlockSpec(blk, lambda b, h: (b, h)),
        out_shape=jax.ShapeDtypeStruct((m, n_heads * HEAD_DIM), BF16),
        scratch_shapes=[pltpu.VMEM((n_chunks, HEAD_DIM, HEAD_DIM), F32),
                        pltpu.VMEM((n_chunks, HEAD_DIM, HEAD_DIM), F32),
                        pltpu.VMEM((n_chunks, HEAD_DIM, HEAD_DIM), BF16),
                        pltpu.VMEM((n_chunks, HEAD_DIM, HEAD_DIM), BF16)],
        compiler_params=_params(blocks, scratch),
        name="retention",
    )(log_decay, z_ret, z_ret, z_ret, z_ret, gn_gain.reshape(1, -1))


def _attn_kernel(slope_ref, qraw_ref, kraw_ref, v_ref, qg_ref, kg_ref, o_ref,
                 q_ref, k_ref, q4, k4, v4, qd, kd, vd, bias_s, out_s, lse_s, *, seq_len):
    bq = ATT_BLOCK
    kw_max = bq + 2 * BRANCH_HALF
    h = pl.program_id(0)

    @pl.when(pl.program_id(1) == 0)
    def _():
        slope = slope_ref[h]
        row = lax.broadcasted_iota(jnp.int32, (bq, kw_max), 0)
        col = lax.broadcasted_iota(jnp.int32, (bq, kw_max), 1)
        for bi, d in enumerate(BRANCH_DILATIONS):
            kw = min(kw_max, seq_len // d)
            for var, off in enumerate((0, -BRANCH_HALF, bq - kw)):
                dist = jnp.abs(col - row + off)
                bias_s[bi, var] = jnp.where(dist <= BRANCH_HALF,
                                            dist.astype(F32) * (-slope * float(d)), NEG_BIG)

    q = qraw_ref[...]
    q_ref[...] = (q * lax.rsqrt(jnp.mean(q * q, axis=-1, keepdims=True) + RMS_EPS)
                  * qg_ref[...]) * (HEAD_DIM ** -0.5)
    k = kraw_ref[...]
    k_ref[...] = k * lax.rsqrt(jnp.mean(k * k, axis=-1, keepdims=True) + RMS_EPS) * kg_ref[...]

    m4 = seq_len // 4
    m16 = seq_len // 16
    for d in (4, 16, 1):
        bi = BRANCH_DILATIONS.index(d)
        m = seq_len // d
        nb = m // bq
        kw = min(kw_max, m)
        log_nb = nb.bit_length() - 1
        assert nb >= 1 and (1 << log_nb) == nb

        if d == 4:
            for r in range(4):
                dst = pl.ds(r * m4, m4)
                src = pl.ds(r, m4, stride=4)
                for src_ref, f32_ref, bf_ref in ((q_ref, q4, qd), (k_ref, k4, kd), (v_ref, v4, vd)):
                    val = src_ref[src, :]
                    f32_ref[dst, :] = val
                    bf_ref[dst, :] = val.astype(BF16)
        elif d == 16:
            for r4 in range(4):
                for r1 in range(4):
                    dst = pl.ds((r4 + 4 * r1) * m16, m16)
                    src = pl.ds(r4 * m4 + r1, m16, stride=4)
                    qd[dst, :] = q4[src, :].astype(BF16)
                    kd[dst, :] = k4[src, :].astype(BF16)
                    vd[dst, :] = v4[src, :].astype(BF16)
        else:
            qd[...] = q_ref[...].astype(BF16)
            kd[...] = k_ref[...].astype(BF16)
            vd[...] = v_ref[...].astype(BF16)

        bias_b = bias_s.at[bi]

        def group(tg, carry, d=d, m=m, nb=nb, kw=kw, log_nb=log_nb, bias_b=bias_b):
            geo, scores, probs = [], [], []
            for g in range(ATT_GROUP):
                t = tg * ATT_GROUP + g
                tl = t & (nb - 1)
                q0 = tl * bq
                k0 = jnp.clip(q0 - BRANCH_HALF, 0, m - kw)
                kstart = pl.multiple_of((t - tl) * bq + k0, BRANCH_HALF)
                var = jnp.where(tl == 0, 0, jnp.where(tl == nb - 1, 2, 1))
                geo.append((t, q0, kstart, var))
                qb = qd[pl.ds(pl.multiple_of(t * bq, bq), bq), :]
                kb = kd[pl.ds(kstart, kw), :]
                scores.append(lax.dot_general(qb, kb, (((1,), (1,)), ((), ())),
                                              preferred_element_type=F32))
            for g in range(ATT_GROUP):
                s = scores[g] + bias_b[geo[g][3], :, 0:kw]
                mx = jnp.max(s, axis=-1, keepdims=True)
                e = jnp.exp(s - mx)
                den = jnp.sum(e, axis=-1, keepdims=True)
                probs.append((e.astype(BF16), mx, den))
            for g in range(ATT_GROUP):
                t, q0, kstart, _ = geo[g]
                e, mx, den = probs[g]
                acc = jnp.dot(e, vd[pl.ds(kstart, kw), :], preferred_element_type=F32)
                if d == 1:
                    rows = pl.ds(pl.multiple_of(t * bq, bq), bq)
                    l4, l16 = lse_s[0, rows, :], lse_s[1, rows, :]
                    shift = jnp.maximum(jnp.maximum(l4, l16), mx)
                    w1 = jnp.exp(mx - shift)
                    w4 = jnp.exp(l4 - shift)
                    w16 = jnp.exp(l16 - shift)
                    num = w1 * acc + w4 * out_s[0, rows, :] + w16 * out_s[1, rows, :]
                    o_ref[rows, :] = (num / (w1 * den + w4 + w16)).astype(o_ref.dtype)
                else:
                    slot = 0 if d == 4 else 1
                    dst = pl.ds((t >> log_nb) + d * q0, bq, stride=d)
                    out_s[slot, dst, :] = acc / den
                    lse_s[slot, dst, :] = jnp.broadcast_to(mx + jnp.log(den), (bq, HEAD_DIM))
            return carry

        lax.fori_loop(0, seq_len // (bq * ATT_GROUP), group, 0)


def _alibi_slopes(n):
    return jnp.asarray(2.0 ** (-8.0 * np.arange(1, n + 1) / n), F32)


def _attention(z_att, q_gain, k_gain, batch, seq_len, n_heads=N_HEADS):
    m = batch * seq_len
    assert (seq_len // ATT_BLOCK) % ATT_GROUP == 0
    assert seq_len // max(BRANCH_DILATIONS) >= ATT_BLOCK
    blk = (seq_len, HEAD_DIM)
    bias_shape = (len(BRANCH_DILATIONS), 3, ATT_BLOCK, ATT_BLOCK + 2 * BRANCH_HALF)
    blocks = 3 * _nbytes(blk, F32) + _nbytes(blk, BF16) + 2 * _nbytes((1, HEAD_DIM), F32)
    scratch = (5 * _nbytes(blk, F32) + 3 * _nbytes(blk, BF16) + _nbytes(bias_shape, F32)
               + 2 * _nbytes((2,) + blk, F32))

    def col(c):
        return lambda h, b: (b, h + c * n_heads)

    return pl.pallas_call(
        functools.partial(_attn_kernel, seq_len=seq_len),
        grid=(n_heads, batch),
        in_specs=[pl.BlockSpec(memory_space=pltpu.SMEM),
                  pl.BlockSpec(blk, col(0)), pl.BlockSpec(blk, col(1)), pl.BlockSpec(blk, col(2)),
                  pl.BlockSpec((1, HEAD_DIM), lambda h, b: (0, 0)),
                  pl.BlockSpec((1, HEAD_DIM), lambda h, b: (0, 0))],
        out_specs=pl.BlockSpec(blk, lambda h, b: (b, h)),
        out_shape=jax.ShapeDtypeStruct((m, n_heads * HEAD_DIM), BF16),
        scratch_shapes=[pltpu.VMEM(blk, F32)] * 5
                       + [pltpu.VMEM(blk, BF16)] * 3
                       + [pltpu.VMEM(bias_shape, F32),
                          pltpu.VMEM((2,) + blk, F32), pltpu.VMEM((2,) + blk, F32)],
        compiler_params=_params(blocks, scratch),
        name="dilated_attention",
    )(_alibi_slopes(n_heads), z_att, z_att, z_att,
      q_gain.reshape(1, -1), k_gain.reshape(1, -1))


def _residual_outputs(new_rows, x_ref, xb_ref, ssq_ref):
    parts = []
    for rows in _row_chunks(x_ref.shape[0]):
        x_new = new_rows(rows)
        x_ref[rows, :] = x_new
        xb_ref[rows, :] = x_new.astype(BF16)
        parts.append(_lane_partial_ssq(x_new))
    part = jnp.concatenate(parts, axis=0)
    j = pl.program_id(1)

    @pl.when(j == 0)
    def _():
        ssq_ref[...] = part

    @pl.when(j > 0)
    def _():
        ssq_ref[...] += part


def _residual_specs(m, tm, tn):
    specs = [pl.BlockSpec((tm, tn), lambda i, j: (i, j)),
             pl.BlockSpec((tm, tn), lambda i, j: (i, j)),
             pl.BlockSpec((tm, LANES), lambda i, j: (i, 0))]
    shapes = [jax.ShapeDtypeStruct((m, D_MODEL), F32),
              jax.ShapeDtypeStruct((m, D_MODEL), BF16),
              jax.ShapeDtypeStruct((m, LANES), F32)]
    nbytes = _nbytes((tm, tn), F32) + _nbytes((tm, tn), BF16) + _nbytes((tm, LANES), F32)
    return specs, shapes, nbytes


def _mm_out_kernel(ret_ref, att_ref, wt_ref, wb_ref, x_ref, xo_ref, xb_ref, ssq_ref):
    def new_rows(rows):
        acc = jnp.dot(ret_ref[rows, :], wt_ref[...], preferred_element_type=F32)
        acc = acc + jnp.dot(att_ref[rows, :], wb_ref[...], preferred_element_type=F32)
        return x_ref[rows, :] + acc

    _residual_outputs(new_rows, xo_ref, xb_ref, ssq_ref)


def _mm_out(ret, att, w, x, tm=1024, tn=512, jobs=()):
    m = x.shape[0]
    kh = MIX_HALF
    out_specs, out_shape, out_bytes = _residual_specs(m, tm, tn)
    blocks = (2 * _nbytes((tm, kh), BF16) + 2 * _nbytes((kh, tn), BF16) + _nbytes((tm, tn), F32)
              + out_bytes)
    return _mm_call(
        _mm_out_kernel,
        grid=(m // tm, D_MODEL // tn),
        in_specs=[pl.BlockSpec((tm, kh), lambda i, j: (i, 0)),
                  pl.BlockSpec((tm, kh), lambda i, j: (i, 0)),
                  pl.BlockSpec((kh, tn), lambda i, j: (0, j)),
                  pl.BlockSpec((kh, tn), lambda i, j: (1, j)),
                  pl.BlockSpec((tm, tn), lambda i, j: (i, j))],
        out_specs=out_specs, out_shape=out_shape,
        operands=[ret, att, w, w, x],
        block_bytes=blocks, temp_bytes=2 * _nbytes((tm, tn), F32), name="mm_out", jobs=jobs)


def _mm_gateup_kernel(xb_ref, ssq_ref, wg_ref, wu_ref, h_ref):
    for rows in _row_chunks(xb_ref.shape[0]):
        r = _row_scale(ssq_ref, rows)
        xb = xb_ref[rows, :]
        a = jnp.dot(xb, wg_ref[...], preferred_element_type=F32) * r
        u = jnp.dot(xb, wu_ref[...], preferred_element_type=F32) * r
        h_ref[rows, :] = (a * _sigmoid(a) * u).astype(h_ref.dtype)


def _mm_gateup(xb, ssq, wg, wu, tm=1024, tn=512, jobs=()):
    m, k = xb.shape
    s = ssq.shape[1]
    n = wg.shape[1]
    blocks = (_nbytes((tm, k), BF16) + _nbytes((tm, s), F32) + 2 * _nbytes((k, tn), BF16)
              + _nbytes((tm, tn), BF16))
    (hid,), cast = _mm_call(
        _mm_gateup_kernel,
        grid=(m // tm, pl.cdiv(n, tn)),
        in_specs=[pl.BlockSpec((tm, k), lambda i, j: (i, 0)),
                  pl.BlockSpec((tm, s), lambda i, j: (i, 0)),
                  pl.BlockSpec((k, tn), lambda i, j: (0, j)),
                  pl.BlockSpec((k, tn), lambda i, j: (0, j))],
        out_specs=[pl.BlockSpec((tm, tn), lambda i, j: (i, j))],
        out_shape=[jax.ShapeDtypeStruct((m, n), BF16)],
        operands=[xb, ssq, wg, wu],
        block_bytes=blocks, temp_bytes=4 * _nbytes((tm, tn), F32), name="mm_gateup", jobs=jobs)
    return hid, cast


def _mm_down_kernel(h_ref, w_ref, x_ref, xo_ref, xb_ref, ssq_ref):
    def new_rows(rows):
        return x_ref[rows, :] + jnp.dot(h_ref[rows, :], w_ref[...], preferred_element_type=F32)

    _residual_outputs(new_rows, xo_ref, xb_ref, ssq_ref)


def _mm_down(hid, w, x, tm=512, tn=512, jobs=()):
    m, k = hid.shape
    out_specs, out_shape, out_bytes = _residual_specs(m, tm, tn)
    blocks = _nbytes((tm, k), BF16) + _nbytes((k, tn), BF16) + _nbytes((tm, tn), F32) + out_bytes
    return _mm_call(
        _mm_down_kernel,
        grid=(m // tm, D_MODEL // tn),
        in_specs=[pl.BlockSpec((tm, k), lambda i, j: (i, 0)),
                  pl.BlockSpec((k, tn), lambda i, j: (0, j)),
                  pl.BlockSpec((tm, tn), lambda i, j: (i, j))],
        out_specs=out_specs, out_shape=out_shape,
        operands=[hid, w, x],
        block_bytes=blocks, temp_bytes=_nbytes((tm, tn), F32), name="mm_down", jobs=jobs)


def _mm_ple_kernel(xb_ref, ssq_ref, wg_ref, p_ref, wp_ref, x_ref, xo_ref, xb_out_ref, ssq_out_ref):
    def new_rows(rows):
        e = jnp.dot(xb_ref[rows, :], wg_ref[...], preferred_element_type=F32)
        e = e * _row_scale(ssq_ref, rows)
        proj = jnp.dot(p_ref[rows, :].astype(BF16), wp_ref[...], preferred_element_type=F32)
        return x_ref[rows, :] + _sigmoid(e) * proj

    _residual_outputs(new_rows, xo_ref, xb_out_ref, ssq_out_ref)


def _mm_ple(xb, ssq, wg, p, wp, x, tm=1024, tn=512, jobs=()):
    m, k = xb.shape
    s = ssq.shape[1]
    out_specs, out_shape, out_bytes = _residual_specs(m, tm, tn)
    blocks = (_nbytes((tm, k), BF16) + _nbytes((tm, s), F32) + _nbytes((k, tn), BF16)
              + _nbytes((tm, PLE_DIM), F32) + _nbytes((PLE_DIM, tn), BF16)
              + _nbytes((tm, tn), F32) + out_bytes)
    return _mm_call(
        _mm_ple_kernel,
        grid=(m // tm, D_MODEL // tn),
        in_specs=[pl.BlockSpec((tm, k), lambda i, j: (i, 0)),
                  pl.BlockSpec((tm, s), lambda i, j: (i, 0)),
                  pl.BlockSpec((k, tn), lambda i, j: (0, j)),
                  pl.BlockSpec((tm, PLE_DIM), lambda i, j: (i, 0)),
                  pl.BlockSpec((PLE_DIM, tn), lambda i, j: (0, j)),
                  pl.BlockSpec((tm, tn), lambda i, j: (i, j))],
        out_specs=out_specs, out_shape=out_shape,
        operands=[xb, ssq, wg, p, wp, x],
        block_bytes=blocks, temp_bytes=2 * _nbytes((tm, tn), F32), name="mm_ple", jobs=jobs)


_HOSTED = {
    (0, "mm_in_ret"): [("w_out", 0)],
    (0, "mm_in_att"): [("w_gate", 0)],
    (0, "mm_out"): [("w_up", 0)],
    (0, "mm_gateup"): [("w_down", 0), ("w_ple_gate", 0), ("w_in", 1)],
    (0, "mm_down"): [("w_out", 1)],
    (0, "mm_ple"): [("w_up", 1)],
    (1, "mm_in_ret"): [("w_gate", 1)],
    (1, "mm_gateup"): [("w_down", 1), ("w_ple_gate", 1)],
}
_GAIN_OF = {"w_in": "ln_mix", "w_gate": "ln_ffn", "w_up": "ln_ffn", "w_ple_gate": "ln_ple"}
_ROW_BLOCK = {"w_in": 32, "w_out": 32, "w_gate": 64, "w_up": 64, "w_down": 64, "w_ple_gate": 32}
_ROW_BLOCK_OVERRIDE = {(0, "mm_in_ret", "w_out"): 64}


def _run_trunk(x, p, weights, raw, host):
    batch, seq_len, _ = x.shape
    m = batch * seq_len
    x = x.reshape(m, D_MODEL)
    xb, ssq = _prep(x)

    def jobs_for(layer, call):
        if not host:
            return [], []
        names = _HOSTED.get((layer, call), [])
        jobs = []
        for name, wl in names:
            gain = raw[_GAIN_OF[name]][..., None] if name in _GAIN_OF else None
            rb = _ROW_BLOCK_OVERRIDE.get((layer, call, name), _ROW_BLOCK[name])
            jobs.append((raw[name], wl, gain, rb))
        return names, jobs

    def store(names, cast):
        for (name, wl), w in zip(names, cast):
            weights[wl][name] = w

    for i in range(len(weights)):
        lw = weights[i]
        names, jobs = jobs_for(i, "mm_in_ret")
        z_ret, cast = _mm_in(xb, ssq, lw["w_in"], 0, 4 * MIX_HALF, BF16, "mm_in_ret", tn=1024,
                             jobs=jobs)
        store(names, cast)
        names, jobs = jobs_for(i, "mm_in_att")
        z_att, cast = _mm_in(xb, ssq, lw["w_in"], 4 * MIX_HALF, 3 * MIX_HALF, F32, "mm_in_att",
                             tn=512 if jobs else 1024, jobs=jobs)
        store(names, cast)
        ret = _retention(z_ret, raw["ret_log_decay"][i], raw["ret_gn"][i], batch, seq_len)
        att = _attention(z_att, raw["q_norm"][i], raw["k_norm"][i], batch, seq_len)
        names, jobs = jobs_for(i, "mm_out")
        (x, xb, ssq), cast = _mm_out(ret, att, lw["w_out"], x, jobs=jobs)
        store(names, cast)
        names, jobs = jobs_for(i, "mm_gateup")
        hid, cast = _mm_gateup(xb, ssq, lw["w_gate"], lw["w_up"], jobs=jobs)
        store(names, cast)
        names, jobs = jobs_for(i, "mm_down")
        (x, xb, ssq), cast = _mm_down(hid, lw["w_down"], x, jobs=jobs)
        store(names, cast)
        names, jobs = jobs_for(i, "mm_ple")
        (x, xb, ssq), cast = _mm_ple(xb, ssq, lw["w_ple_gate"], p[i].reshape(m, PLE_DIM),
                                     lw["w_ple_proj"], x, jobs=jobs)
        store(names, cast)
    return x.reshape(batch, seq_len, D_MODEL)


def kernel(x_prompt, x_sample, p_prompt, p_sample, ln_mix, w_in, ret_log_decay, ret_gn, q_norm,
           k_norm, w_out, ln_ffn, w_gate, w_up, w_down, ln_ple, w_ple_gate, w_ple_proj):
    depth = w_in.shape[0]
    raw = dict(ln_mix=ln_mix, w_in=w_in, ret_log_decay=ret_log_decay, ret_gn=ret_gn, q_norm=q_norm,
               k_norm=k_norm, w_out=w_out, ln_ffn=ln_ffn, w_gate=w_gate, w_up=w_up, w_down=w_down,
               ln_ple=ln_ple, w_ple_gate=w_ple_gate, w_ple_proj=w_ple_proj)
    hosted = {(name, wl) for names in _HOSTED.values() for name, wl in names} if depth == 2 else set()
    weights = []
    for i in range(depth):
        lw = {"w_ple_proj": w_ple_proj[i].astype(BF16)}
        for name in ("w_in", "w_out", "w_gate", "w_up", "w_down", "w_ple_gate"):
            if (name, i) in hosted:
                continue
            w = raw[name][i]
            if name in _GAIN_OF:
                w = raw[_GAIN_OF[name]][i][:, None] * w
            lw[name] = w.astype(BF16)
        weights.append(lw)
    y_prompt = _run_trunk(x_prompt, p_prompt, weights, raw, host=depth == 2)
    y_sample = _run_trunk(x_sample, p_sample, weights, raw, host=False)
    return (y_prompt, y_sample)
```

```python
import functools

import jax
import jax.numpy as jnp
import numpy as np
from jax import lax
from jax.experimental import pallas as pl
from jax.experimental.pallas import tpu as pltpu

D_MODEL = 4096
HEAD_DIM = 128
N_HEADS = 16
MIX_HALF = N_HEADS * HEAD_DIM
PLE_DIM = 256
RMS_EPS = 1e-6
GN_EPS = 1e-5
NEG_BIG = -1e30
RET_CHUNK = 128
RET_GROUP = 16
BRANCH_DILATIONS = (1, 4, 16)
BRANCH_HALF = 64
LANES = 128
ATT_BLOCK = 128
ATT_GROUP_MIN, ATT_GROUP_MAX = 8, 16
MM_ROW_SPLIT = 1

F32 = jnp.float32
BF16 = jnp.bfloat16
VMEM_HEADROOM = 5 * 1024 * 1024


def _nbytes(shape, dtype):
    return int(np.prod(shape)) * jnp.dtype(dtype).itemsize


def _params(block_bytes, scratch_bytes=0, n_grid=2):
    limit = 2 * block_bytes + scratch_bytes + VMEM_HEADROOM
    return pltpu.CompilerParams(
        dimension_semantics=("arbitrary",) * n_grid,
        vmem_limit_bytes=int(limit),
    )


def _row_chunks(tm):
    nr = tm // MM_ROW_SPLIT
    return [pl.ds(c * nr, nr) for c in range(MM_ROW_SPLIT)]


def _row_scale(ssq_ref, rows=slice(None)):
    tot = jnp.sum(ssq_ref[rows, :], axis=-1, keepdims=True)
    return lax.rsqrt(tot * (1.0 / D_MODEL) + RMS_EPS)


def _sigmoid(x):
    return 0.5 * jnp.tanh(0.5 * x) + 0.5


def _lane_partial_ssq(x):
    sq = x * x
    acc = sq[:, 0:LANES]
    for c in range(1, x.shape[1] // LANES):
        acc = acc + sq[:, c * LANES:(c + 1) * LANES]
    return acc


def _mm_call(kernel_fn, *, grid, in_specs, out_specs, out_shape, operands, block_bytes,
             temp_bytes, name, jobs=()):
    n_in, n_out = len(in_specs), len(out_specs)
    gj = grid[1]
    steps = grid[0] * grid[1]
    in_specs, out_specs, out_shape = list(in_specs), list(out_specs), list(out_shape)
    operands = list(operands)
    job_in_specs, job_operands, job_out_specs, job_out_shape, job_kind = [], [], [], [], []
    for src, layer, gain, rb, want_ssq in jobs:
        _, r, c = src.shape
        nbk = r // rb
        assert r % rb == 0 and nbk <= steps, (name, r, rb, steps)

        def blk(i, j, nbk=nbk):
            return jnp.minimum(i * gj + j, nbk - 1)

        job_in_specs.append(pl.BlockSpec((None, rb, c),
                                         lambda i, j, blk=blk, layer=layer: (layer, blk(i, j), 0)))
        job_operands.append(src)
        if gain is not None:
            job_in_specs.append(pl.BlockSpec((None, rb, 1),
                                             lambda i, j, blk=blk, layer=layer: (layer, blk(i, j), 0)))
            job_operands.append(gain)
        job_kind.append((gain is not None, want_ssq))
        job_out_specs.append(pl.BlockSpec((rb, c), lambda i, j, blk=blk: (blk(i, j), 0)))
        job_out_shape.append(jax.ShapeDtypeStruct((r, c), BF16))
        if want_ssq:
            job_out_specs.append(pl.BlockSpec((rb, LANES), lambda i, j, blk=blk: (blk(i, j), 0)))
            job_out_shape.append(jax.ShapeDtypeStruct((r, LANES), F32))
        block_bytes += (_nbytes((rb, c), F32) + _nbytes((rb, c), BF16)
                        + 2 * _nbytes((rb, LANES), F32))

    def body(*refs):
        main_in = refs[:n_in]
        pos = n_in
        job_in = []
        for hg, _ in job_kind:
            job_in.append((refs[pos], refs[pos + 1] if hg else None))
            pos += 2 if hg else 1
        main_out = refs[pos:pos + n_out]
        pos += n_out
        for (src_ref, gain_ref), (_, want_ssq) in zip(job_in, job_kind):
            w = src_ref[...]
            if gain_ref is not None:
                w = w * gain_ref[...]
            refs[pos][...] = w.astype(BF16)
            pos += 1
            if want_ssq:
                refs[pos][...] = _lane_partial_ssq(w)
                pos += 1
        kernel_fn(*main_in, *main_out)

    res = pl.pallas_call(
        body,
        grid=grid,
        in_specs=in_specs + job_in_specs,
        out_specs=out_specs + job_out_specs,
        out_shape=out_shape + job_out_shape,
        compiler_params=_params(block_bytes, temp_bytes),
        name=name,
    )(*operands, *job_operands)
    return tuple(res[:n_out]), tuple(res[n_out:])


def _prep_kernel(x_ref, xb_ref, ssq_ref):
    x = x_ref[...]
    xb_ref[...] = x.astype(BF16)
    ssq_ref[...] = _lane_partial_ssq(x)


def _prep(x):
    m = x.shape[0]
    tm = 512
    blocks = _nbytes((tm, D_MODEL), F32) + _nbytes((tm, D_MODEL), BF16) + _nbytes((tm, LANES), F32)
    return pl.pallas_call(
        _prep_kernel,
        grid=(m // tm,),
        in_specs=[pl.BlockSpec((tm, D_MODEL), lambda i: (i, 0))],
        out_specs=[pl.BlockSpec((tm, D_MODEL), lambda i: (i, 0)),
                   pl.BlockSpec((tm, LANES), lambda i: (i, 0))],
        out_shape=[jax.ShapeDtypeStruct((m, D_MODEL), BF16),
                   jax.ShapeDtypeStruct((m, LANES), F32)],
        compiler_params=_params(blocks, n_grid=1),
        name="prep",
    )(x)


def _mm_in_kernel(xb_ref, ssq_ref, w_ref, o_ref):
    for rows in _row_chunks(xb_ref.shape[0]):
        acc = jnp.dot(xb_ref[rows, :], w_ref[...], preferred_element_type=F32)
        o_ref[rows, :] = (acc * _row_scale(ssq_ref, rows)).astype(o_ref.dtype)


def _mm_in(xb, ssq, w, col0, ncols, out_dtype, name, tm=1024, tn=512, jobs=()):
    m, k = xb.shape
    s = ssq.shape[1]
    jb0 = col0 // tn
    blocks = (_nbytes((tm, k), BF16) + _nbytes((tm, s), F32) + _nbytes((k, tn), BF16)
              + _nbytes((tm, tn), out_dtype))
    (z,), cast = _mm_call(
        _mm_in_kernel,
        grid=(m // tm, ncols // tn),
        in_specs=[pl.BlockSpec((tm, k), lambda i, j: (i, 0)),
                  pl.BlockSpec((tm, s), lambda i, j: (i, 0)),
                  pl.BlockSpec((k, tn), lambda i, j: (0, j + jb0))],
        out_specs=[pl.BlockSpec((tm, tn), lambda i, j: (i, j))],
        out_shape=[jax.ShapeDtypeStruct((m, ncols), out_dtype)],
        operands=[xb, ssq, w],
        block_bytes=blocks, temp_bytes=_nbytes((tm, tn), F32), name=name, jobs=jobs)
    return z, cast


def _ret_kernel(ld_ref, q_ref, k_ref, v_ref, g_ref, gn_ref, o_ref,
                kvf_s, kvb_s, sf_s, sb_s, *, seq_len):
    c = RET_CHUNK
    n_chunks = seq_len // c
    h = pl.program_id(1)
    lf = ld_ref[0, h]
    lb = ld_ref[1, h]
    scale = HEAD_DIM ** -0.5

    ii = lax.broadcasted_iota(jnp.int32, (c, c), 0)
    jj = lax.broadcasted_iota(jnp.int32, (c, c), 1)
    dist = (ii - jj).astype(F32)
    dmat = jnp.where(ii >= jj,
                     jnp.exp(lf * jnp.maximum(dist, 0.0)),
                     jnp.exp(lb * jnp.maximum(-dist, 0.0))) * scale
    pos = lax.broadcasted_iota(jnp.int32, (c, 1), 0).astype(F32)
    zeta_f = jnp.exp(lf * (c - 1.0 - pos)) * scale
    zeta_b = jnp.exp(lb * pos) * scale
    xi_f = jnp.exp(lf * (pos + 1.0))
    xi_b = jnp.exp(lb * (c - pos))
    one = jnp.ones((1, 1), F32)
    cd_f = jnp.exp(one * (lf * c))
    cd_b = jnp.exp(one * (lb * c))

    tdims = (((0,), (0,)), ((), ()))

    def rows(n):
        return pl.ds(pl.multiple_of(n * c, c), c)

    def summaries(n, carry):
        k = k_ref[rows(n), :]
        v = v_ref[rows(n), :].astype(F32)
        vz = jnp.concatenate([(v * zeta_f).astype(BF16), (v * zeta_b).astype(BF16)], axis=1)
        kv = lax.dot_general(k, vz, tdims, preferred_element_type=F32)
        kvf_s[n] = kv[:, :HEAD_DIM]
        kvb_s[n] = kv[:, HEAD_DIM:]
        return carry

    lax.fori_loop(0, n_chunks, summaries, 0, unroll=8)

    def scan_f(n, st):
        sf_s[n] = st.astype(BF16)
        return st * cd_f + kvf_s[n]

    lax.fori_loop(0, n_chunks, scan_f, jnp.zeros((HEAD_DIM, HEAD_DIM), F32))

    def scan_b(t, st):
        n = n_chunks - 1 - t
        sb_s[n] = st.astype(BF16)
        return st * cd_b + kvb_s[n]

    lax.fori_loop(0, n_chunks, scan_b, jnp.zeros((HEAD_DIM, HEAD_DIM), F32))

    gain = gn_ref[...]

    def outputs(ng, carry):
        scores, crosses = [], []
        for j in range(RET_GROUP):
            n = ng * RET_GROUP + j
            q = q_ref[rows(n), :]
            k = k_ref[rows(n), :]
            scores.append(lax.dot_general(q, k, (((1,), (1,)), ((), ())),
                                          preferred_element_type=F32))
            st = jnp.concatenate([sf_s[n], sb_s[n]], axis=1)
            crosses.append(jnp.dot(q, st, preferred_element_type=F32))
        probs = [(s * dmat).astype(BF16) for s in scores]
        for j in range(RET_GROUP):
            n = ng * RET_GROUP + j
            intra = jnp.dot(probs[j], v_ref[rows(n), :], preferred_element_type=F32)
            cross = crosses[j]
            y = intra + cross[:, :HEAD_DIM] * xi_f + cross[:, HEAD_DIM:] * xi_b
            mu = jnp.mean(y, axis=-1, keepdims=True)
            yc = y - mu
            var = jnp.mean(yc * yc, axis=-1, keepdims=True)
            yn = yc * lax.rsqrt(var + GN_EPS) * gain
            g = g_ref[rows(n), :].astype(F32)
            o_ref[rows(n), :] = (g * _sigmoid(g) * yn).astype(o_ref.dtype)
        return carry

    lax.fori_loop(0, n_chunks // RET_GROUP, outputs, 0)


def _retention(z_ret, log_decay, gn_gain, batch, seq_len, n_heads=N_HEADS):
    m = batch * seq_len
    n_chunks = seq_len // RET_CHUNK
    assert seq_len % (RET_CHUNK * RET_GROUP) == 0
    blk = (seq_len, HEAD_DIM)
    blocks = 5 * _nbytes(blk, BF16) + _nbytes((1, HEAD_DIM), F32)
    scratch = (2 * _nbytes((n_chunks, HEAD_DIM, HEAD_DIM), F32)
               + 2 * _nbytes((n_chunks, HEAD_DIM, HEAD_DIM), BF16))

    def col(c):
        return lambda b, h: (b, h + c * n_heads)

    return pl.pallas_call(
        functools.partial(_ret_kernel, seq_len=seq_len),
        grid=(batch, n_heads),
        in_specs=[pl.BlockSpec(memory_space=pltpu.SMEM),
                  pl.BlockSpec(blk, col(0)), pl.BlockSpec(blk, col(1)),
                  pl.BlockSpec(blk, col(2)), pl.BlockSpec(blk, col(3)),
                  pl.BlockSpec((1, HEAD_DIM), lambda b, h: (0, h))],
        out_specs=pl.BlockSpec(blk, lambda b, h: (b, h)),
        out_shape=jax.ShapeDtypeStruct((m, n_heads * HEAD_DIM), BF16),
        scratch_shapes=[pltpu.VMEM((n_chunks, HEAD_DIM, HEAD_DIM), F32),
                        pltpu.VMEM((n_chunks, HEAD_DIM, HEAD_DIM), F32),
                        pltpu.VMEM((n_chunks, HEAD_DIM, HEAD_DIM), BF16),
                        pltpu.VMEM((n_chunks, HEAD_DIM, HEAD_DIM), BF16)],
        compiler_params=_params(blocks, scratch),
        name="retention",
    )(log_decay, z_ret, z_ret, z_ret, z_ret, gn_gain.reshape(1, -1))


def _attn_kernel(slope_ref, qraw_ref, kraw_ref, v_ref, qg_ref, kg_ref, o_ref,
                 q_ref, k_ref, q4, k4, v4, qd, kd, vd, bias_s, out_s, lse_s, *, seq_len):
    bq = ATT_BLOCK
    kw_max = bq + 2 * BRANCH_HALF
    h = pl.program_id(0)

    @pl.when(pl.program_id(1) == 0)
    def _():
        slope = slope_ref[h]
        row = lax.broadcasted_iota(jnp.int32, (bq, kw_max), 0)
        col = lax.broadcasted_iota(jnp.int32, (bq, kw_max), 1)
        for bi, d in enumerate(BRANCH_DILATIONS):
            kw = min(kw_max, seq_len // d)
            for var, off in enumerate((0, -BRANCH_HALF, bq - kw)):
                dist = jnp.abs(col - row + off)
                bias_s[bi, var] = jnp.where(dist <= BRANCH_HALF,
                                            dist.astype(F32) * (-slope * float(d)), NEG_BIG)

    q = qraw_ref[...]
    q_ref[...] = (q * lax.rsqrt(jnp.mean(q * q, axis=-1, keepdims=True) + RMS_EPS)
                  * qg_ref[...]) * (HEAD_DIM ** -0.5)
    k = kraw_ref[...]
    k_ref[...] = k * lax.rsqrt(jnp.mean(k * k, axis=-1, keepdims=True) + RMS_EPS) * kg_ref[...]

    m4 = seq_len // 4
    m16 = seq_len // 16
    for d in (4, 16, 1):
        bi = BRANCH_DILATIONS.index(d)
        m = seq_len // d
        nb = m // bq
        kw = min(kw_max, m)
        log_nb = nb.bit_length() - 1
        assert nb >= 1 and (1 << log_nb) == nb

        if d == 4:
            for r in range(4):
                dst = pl.ds(r * m4, m4)
                src = pl.ds(r, m4, stride=4)
                for src_ref, f32_ref, bf_ref in ((q_ref, q4, qd), (k_ref, k4, kd), (v_ref, v4, vd)):
                    val = src_ref[src, :]
                    f32_ref[dst, :] = val
                    bf_ref[dst, :] = val.astype(BF16)
        elif d == 16:
            for r4 in range(4):
                for r1 in range(4):
                    dst = pl.ds((r4 + 4 * r1) * m16, m16)
                    src = pl.ds(r4 * m4 + r1, m16, stride=4)
                    qd[dst, :] = q4[src, :].astype(BF16)
                    kd[dst, :] = k4[src, :].astype(BF16)
                    vd[dst, :] = v4[src, :].astype(BF16)
        else:
            qd[...] = q_ref[...].astype(BF16)
            kd[...] = k_ref[...].astype(BF16)
            vd[...] = v_ref[...].astype(BF16)

        bias_b = bias_s.at[bi]

        grp = _att_group(seq_len)

        def group(tg, carry, d=d, m=m, nb=nb, kw=kw, log_nb=log_nb, bias_b=bias_b, grp=grp):
            geo, scores, probs = [], [], []
            for g in range(grp):
                t = tg * grp + g
                tl = t & (nb - 1)
                q0 = tl * bq
                k0 = jnp.clip(q0 - BRANCH_HALF, 0, m - kw)
                kstart = pl.multiple_of((t - tl) * bq + k0, BRANCH_HALF)
                var = jnp.where(tl == 0, 0, jnp.where(tl == nb - 1, 2, 1))
                geo.append((t, q0, kstart, var))
                qb = qd[pl.ds(pl.multiple_of(t * bq, bq), bq), :]
                kb = kd[pl.ds(kstart, kw), :]
                scores.append(lax.dot_general(qb, kb, (((1,), (1,)), ((), ())),
                                              preferred_element_type=F32))
            for g in range(grp):
                s = scores[g] + bias_b[geo[g][3], :, 0:kw]
                mx = jnp.max(s, axis=-1, keepdims=True)
                e = jnp.exp(s - mx)
                den = jnp.sum(e, axis=-1, keepdims=True)
                probs.append((e.astype(BF16), mx, den))
            for g in range(grp):
                t, q0, kstart, _ = geo[g]
                e, mx, den = probs[g]
                acc = jnp.dot(e, vd[pl.ds(kstart, kw), :], preferred_element_type=F32)
                if d == 1:
                    rows = pl.ds(pl.multiple_of(t * bq, bq), bq)
                    l4, l16 = lse_s[0, rows, :], lse_s[1, rows, :]
                    shift = jnp.maximum(jnp.maximum(l4, l16), mx)
                    w1 = jnp.exp(mx - shift)
                    w4 = jnp.exp(l4 - shift)
                    w16 = jnp.exp(l16 - shift)
                    num = w1 * acc + w4 * out_s[0, rows, :] + w16 * out_s[1, rows, :]
                    o_ref[rows, :] = (num / (w1 * den + w4 + w16)).astype(o_ref.dtype)
                else:
                    slot = 0 if d == 4 else 1
                    dst = pl.ds((t >> log_nb) + d * q0, bq, stride=d)
                    out_s[slot, dst, :] = acc / den
                    lse_s[slot, dst, :] = jnp.broadcast_to(mx + jnp.log(den), (bq, HEAD_DIM))
            return carry

        lax.fori_loop(0, seq_len // (bq * grp), group, 0)


def _att_group(seq_len):
    return max(ATT_GROUP_MIN, min(ATT_GROUP_MAX, seq_len // (2 * ATT_BLOCK)))


def _alibi_slopes(n):
    return jnp.asarray(2.0 ** (-8.0 * np.arange(1, n + 1) / n), F32)


def _attention(z_att, q_gain, k_gain, batch, seq_len, n_heads=N_HEADS):
    m = batch * seq_len
    assert (seq_len // ATT_BLOCK) % _att_group(seq_len) == 0
    assert seq_len // max(BRANCH_DILATIONS) >= ATT_BLOCK
    blk = (seq_len, HEAD_DIM)
    bias_shape = (len(BRANCH_DILATIONS), 3, ATT_BLOCK, ATT_BLOCK + 2 * BRANCH_HALF)
    blocks = 3 * _nbytes(blk, F32) + _nbytes(blk, BF16) + 2 * _nbytes((1, HEAD_DIM), F32)
    scratch = (5 * _nbytes(blk, F32) + 3 * _nbytes(blk, BF16) + _nbytes(bias_shape, F32)
               + 2 * _nbytes((2,) + blk, F32))

    def col(c):
        return lambda h, b: (b, h + c * n_heads)

    return pl.pallas_call(
        functools.partial(_attn_kernel, seq_len=seq_len),
        grid=(n_heads, batch),
        in_specs=[pl.BlockSpec(memory_space=pltpu.SMEM),
                  pl.BlockSpec(blk, col(0)), pl.BlockSpec(blk, col(1)), pl.BlockSpec(blk, col(2)),
                  pl.BlockSpec((1, HEAD_DIM), lambda h, b: (0, 0)),
                  pl.BlockSpec((1, HEAD_DIM), lambda h, b: (0, 0))],
        out_specs=pl.BlockSpec(blk, lambda h, b: (b, h)),
        out_shape=jax.ShapeDtypeStruct((m, n_heads * HEAD_DIM), BF16),
        scratch_shapes=[pltpu.VMEM(blk, F32)] * 5
                       + [pltpu.VMEM(blk, BF16)] * 3
                       + [pltpu.VMEM(bias_shape, F32),
                          pltpu.VMEM((2,) + blk, F32), pltpu.VMEM((2,) + blk, F32)],
        compiler_params=_params(blocks, scratch),
        name="dilated_attention",
    )(_alibi_slopes(n_heads), z_att, z_att, z_att,
      q_gain.reshape(1, -1), k_gain.reshape(1, -1))


def _residual_outputs(new_rows, x_ref, xb_ref, ssq_ref):
    parts = []
    for rows in _row_chunks(x_ref.shape[0]):
        x_new = new_rows(rows)
        x_ref[rows, :] = x_new
        xb_ref[rows, :] = x_new.astype(BF16)
        parts.append(_lane_partial_ssq(x_new))
    part = jnp.concatenate(parts, axis=0)
    j = pl.program_id(1)

    @pl.when(j == 0)
    def _():
        ssq_ref[...] = part

    @pl.when(j > 0)
    def _():
        ssq_ref[...] += part


def _residual_specs(m, tm, tn):
    specs = [pl.BlockSpec((tm, tn), lambda i, j: (i, j)),
             pl.BlockSpec((tm, tn), lambda i, j: (i, j)),
             pl.BlockSpec((tm, LANES), lambda i, j: (i, 0))]
    shapes = [jax.ShapeDtypeStruct((m, D_MODEL), F32),
              jax.ShapeDtypeStruct((m, D_MODEL), BF16),
              jax.ShapeDtypeStruct((m, LANES), F32)]
    nbytes = _nbytes((tm, tn), F32) + _nbytes((tm, tn), BF16) + _nbytes((tm, LANES), F32)
    return specs, shapes, nbytes


def _mm_out_kernel(ret_ref, att_ref, wt_ref, wb_ref, x_ref, xo_ref, xb_ref, ssq_ref):
    def new_rows(rows):
        acc = jnp.dot(ret_ref[rows, :], wt_ref[...], preferred_element_type=F32)
        acc = acc + jnp.dot(att_ref[rows, :], wb_ref[...], preferred_element_type=F32)
        return x_ref[rows, :] + acc

    _residual_outputs(new_rows, xo_ref, xb_ref, ssq_ref)


def _mm_out(ret, att, w, x, tm=1024, tn=512, jobs=()):
    m = x.shape[0]
    kh = MIX_HALF
    out_specs, out_shape, out_bytes = _residual_specs(m, tm, tn)
    blocks = (2 * _nbytes((tm, kh), BF16) + 2 * _nbytes((kh, tn), BF16) + _nbytes((tm, tn), F32)
              + out_bytes)
    return _mm_call(
        _mm_out_kernel,
        grid=(m // tm, D_MODEL // tn),
        in_specs=[pl.BlockSpec((tm, kh), lambda i, j: (i, 0)),
                  pl.BlockSpec((tm, kh), lambda i, j: (i, 0)),
                  pl.BlockSpec((kh, tn), lambda i, j: (0, j)),
                  pl.BlockSpec((kh, tn), lambda i, j: (1, j)),
                  pl.BlockSpec((tm, tn), lambda i, j: (i, j))],
        out_specs=out_specs, out_shape=out_shape,
        operands=[ret, att, w, w, x],
        block_bytes=blocks, temp_bytes=2 * _nbytes((tm, tn), F32), name="mm_out", jobs=jobs)


def _mm_gateup_kernel(xb_ref, ssq_ref, wg_ref, wu_ref, h_ref):
    for rows in _row_chunks(xb_ref.shape[0]):
        r = _row_scale(ssq_ref, rows)
        xb = xb_ref[rows, :]
        a = jnp.dot(xb, wg_ref[...], preferred_element_type=F32) * r
        u = jnp.dot(xb, wu_ref[...], preferred_element_type=F32) * r
        h_ref[rows, :] = (a * _sigmoid(a) * u).astype(h_ref.dtype)


def _mm_gateup(xb, ssq, wg, wu, tm=1024, tn=512, jobs=()):
    m, k = xb.shape
    s = ssq.shape[1]
    n = wg.shape[1]
    blocks = (_nbytes((tm, k), BF16) + _nbytes((tm, s), F32) + 2 * _nbytes((k, tn), BF16)
              + _nbytes((tm, tn), BF16))
    (hid,), cast = _mm_call(
        _mm_gateup_kernel,
        grid=(m // tm, pl.cdiv(n, tn)),
        in_specs=[pl.BlockSpec((tm, k), lambda i, j: (i, 0)),
                  pl.BlockSpec((tm, s), lambda i, j: (i, 0)),
                  pl.BlockSpec((k, tn), lambda i, j: (0, j)),
                  pl.BlockSpec((k, tn), lambda i, j: (0, j))],
        out_specs=[pl.BlockSpec((tm, tn), lambda i, j: (i, j))],
        out_shape=[jax.ShapeDtypeStruct((m, n), BF16)],
        operands=[xb, ssq, wg, wu],
        block_bytes=blocks, temp_bytes=4 * _nbytes((tm, tn), F32), name="mm_gateup", jobs=jobs)
    return hid, cast


def _mm_down_kernel(h_ref, w_ref, x_ref, xo_ref, xb_ref, ssq_ref):
    def new_rows(rows):
        return x_ref[rows, :] + jnp.dot(h_ref[rows, :], w_ref[...], preferred_element_type=F32)

    _residual_outputs(new_rows, xo_ref, xb_ref, ssq_ref)


def _mm_down(hid, w, x, tm=512, tn=512, jobs=()):
    m, k = hid.shape
    out_specs, out_shape, out_bytes = _residual_specs(m, tm, tn)
    blocks = _nbytes((tm, k), BF16) + _nbytes((k, tn), BF16) + _nbytes((tm, tn), F32) + out_bytes
    return _mm_call(
        _mm_down_kernel,
        grid=(m // tm, D_MODEL // tn),
        in_specs=[pl.BlockSpec((tm, k), lambda i, j: (i, 0)),
                  pl.BlockSpec((k, tn), lambda i, j: (0, j)),
                  pl.BlockSpec((tm, tn), lambda i, j: (i, j))],
        out_specs=out_specs, out_shape=out_shape,
        operands=[hid, w, x],
        block_bytes=blocks, temp_bytes=_nbytes((tm, tn), F32), name="mm_down", jobs=jobs)


def _mm_ple_kernel(xb_ref, ssq_ref, wg_ref, p_ref, wp_ref, x_ref, xo_ref, xb_out_ref, ssq_out_ref):
    def new_rows(rows):
        e = jnp.dot(xb_ref[rows, :], wg_ref[...], preferred_element_type=F32)
        e = e * _row_scale(ssq_ref, rows)
        proj = jnp.dot(p_ref[rows, :].astype(BF16), wp_ref[...], preferred_element_type=F32)
        return x_ref[rows, :] + _sigmoid(e) * proj

    _residual_outputs(new_rows, xo_ref, xb_out_ref, ssq_out_ref)


def _mm_ple(xb, ssq, wg, p, wp, x, tm=1024, tn=512, jobs=()):
    m, k = xb.shape
    s = ssq.shape[1]
    out_specs, out_shape, out_bytes = _residual_specs(m, tm, tn)
    blocks = (_nbytes((tm, k), BF16) + _nbytes((tm, s), F32) + _nbytes((k, tn), BF16)
              + _nbytes((tm, PLE_DIM), F32) + _nbytes((PLE_DIM, tn), BF16)
              + _nbytes((tm, tn), F32) + out_bytes)
    return _mm_call(
        _mm_ple_kernel,
        grid=(m // tm, D_MODEL // tn),
        in_specs=[pl.BlockSpec((tm, k), lambda i, j: (i, 0)),
                  pl.BlockSpec((tm, s), lambda i, j: (i, 0)),
                  pl.BlockSpec((k, tn), lambda i, j: (0, j)),
                  pl.BlockSpec((tm, PLE_DIM), lambda i, j: (i, 0)),
                  pl.BlockSpec((PLE_DIM, tn), lambda i, j: (0, j)),
                  pl.BlockSpec((tm, tn), lambda i, j: (i, j))],
        out_specs=out_specs, out_shape=out_shape,
        operands=[xb, ssq, wg, p, wp, x],
        block_bytes=blocks, temp_bytes=2 * _nbytes((tm, tn), F32), name="mm_ple", jobs=jobs)


_HOSTED = {
    (0, "mm_in_ret"): [("w_out", 0)],
    (0, "mm_in_att"): [("w_gate", 0)],
    (0, "mm_out"): [("w_up", 0)],
    (0, "mm_gateup"): [("w_down", 0), ("w_ple_gate", 0), ("w_in", 1)],
    (0, "mm_down"): [("w_out", 1)],
    (0, "mm_ple"): [("w_up", 1)],
    (1, "mm_in_ret"): [("w_gate", 1)],
    (1, "mm_out"): [("x_next", 0)],
    (1, "mm_gateup"): [("w_down", 1), ("w_ple_gate", 1)],
}
_NEXT_X = "x_next"
_GAIN_OF = {"w_in": "ln_mix", "w_gate": "ln_ffn", "w_up": "ln_ffn", "w_ple_gate": "ln_ple"}
_ROW_BLOCK = {"w_in": 32, "w_out": 32, "w_gate": 64, "w_up": 64, "w_down": 64, "w_ple_gate": 32,
              "x_next": 128}
_ROW_BLOCK_OVERRIDE = {(0, "mm_in_ret", "w_out"): 64}


def _run_trunk(x, p, weights, raw, host, prepped=None):
    batch, seq_len, _ = x.shape
    m = batch * seq_len
    x = x.reshape(m, D_MODEL)
    xb, ssq = _prep(x) if prepped is None else prepped
    next_prep = []

    def jobs_for(layer, call):
        if not host:
            return [], []
        names = _HOSTED.get((layer, call), [])
        jobs = []
        for name, wl in names:
            gain = raw[_GAIN_OF[name]][..., None] if name in _GAIN_OF else None
            rb = _ROW_BLOCK_OVERRIDE.get((layer, call, name), _ROW_BLOCK[name])
            jobs.append((raw[name], wl, gain, rb, name == _NEXT_X))
        return names, jobs

    def store(names, cast):
        cast = list(cast)
        for name, wl in names:
            if name == _NEXT_X:
                next_prep.extend([cast.pop(0), cast.pop(0)])
            else:
                weights[wl][name] = cast.pop(0)

    for i in range(len(weights)):
        lw = weights[i]
        names, jobs = jobs_for(i, "mm_in_ret")
        z_ret, cast = _mm_in(xb, ssq, lw["w_in"], 0, 4 * MIX_HALF, BF16, "mm_in_ret", tn=1024,
                             jobs=jobs)
        store(names, cast)
        names, jobs = jobs_for(i, "mm_in_att")
        z_att, cast = _mm_in(xb, ssq, lw["w_in"], 4 * MIX_HALF, 3 * MIX_HALF, F32, "mm_in_att",
                             tn=512 if jobs else 1024, jobs=jobs)
        store(names, cast)
        ret = _retention(z_ret, raw["ret_log_decay"][i], raw["ret_gn"][i], batch, seq_len)
        att = _attention(z_att, raw["q_norm"][i], raw["k_norm"][i], batch, seq_len)
        names, jobs = jobs_for(i, "mm_out")
        (x, xb, ssq), cast = _mm_out(ret, att, lw["w_out"], x, jobs=jobs)
        store(names, cast)
        names, jobs = jobs_for(i, "mm_gateup")
        hid, cast = _mm_gateup(xb, ssq, lw["w_gate"], lw["w_up"], jobs=jobs)
        store(names, cast)
        names, jobs = jobs_for(i, "mm_down")
        (x, xb, ssq), cast = _mm_down(hid, lw["w_down"], x, jobs=jobs)
        store(names, cast)
        names, jobs = jobs_for(i, "mm_ple")
        (x, xb, ssq), cast = _mm_ple(xb, ssq, lw["w_ple_gate"], p[i].reshape(m, PLE_DIM),
                                     lw["w_ple_proj"], x, jobs=jobs)
        store(names, cast)
    return x.reshape(batch, seq_len, D_MODEL), tuple(next_prep) or None


def kernel(x_prompt, x_sample, p_prompt, p_sample, ln_mix, w_in, ret_log_decay, ret_gn, q_norm,
           k_norm, w_out, ln_ffn, w_gate, w_up, w_down, ln_ple, w_ple_gate, w_ple_proj):
    depth = w_in.shape[0]
    raw = dict(ln_mix=ln_mix, w_in=w_in, ret_log_decay=ret_log_decay, ret_gn=ret_gn, q_norm=q_norm,
               k_norm=k_norm, w_out=w_out, ln_ffn=ln_ffn, w_gate=w_gate, w_up=w_up, w_down=w_down,
               ln_ple=ln_ple, w_ple_gate=w_ple_gate, w_ple_proj=w_ple_proj,
               x_next=x_sample.reshape(1, -1, D_MODEL))
    hosted = {(name, wl) for names in _HOSTED.values() for name, wl in names} if depth == 2 else set()
    weights = []
    for i in range(depth):
        lw = {"w_ple_proj": w_ple_proj[i].astype(BF16)}
        for name in ("w_in", "w_out", "w_gate", "w_up", "w_down", "w_ple_gate"):
            if (name, i) in hosted:
                continue
            w = raw[name][i]
            if name in _GAIN_OF:
                w = raw[_GAIN_OF[name]][i][:, None] * w
            lw[name] = w.astype(BF16)
        weights.append(lw)
    y_prompt, sample_prep = _run_trunk(x_prompt, p_prompt, weights, raw, host=depth == 2)
    y_sample, _ = _run_trunk(x_sample, p_sample, weights, raw, host=False, prepped=sample_prep)
    return (y_prompt, y_sample)
```

```python
import functools

import jax
import jax.numpy as jnp
import numpy as np
from jax import lax
from jax.experimental import pallas as pl
from jax.experimental.pallas import tpu as pltpu

D_MODEL = 4096
HEAD_DIM = 128
N_HEADS = 16
MIX_HALF = N_HEADS * HEAD_DIM
PLE_DIM = 256
RMS_EPS = 1e-6
GN_EPS = 1e-5
NEG_BIG = -1e30
RET_CHUNK = 128
RET_GROUP = 32
BRANCH_DILATIONS = (1, 4, 16)
BRANCH_HALF = 64
LANES = 128
ATT_BLOCK = 128
ATT_GROUP_MIN, ATT_GROUP_MAX = 8, 16
FF_TILE = 512

F32 = jnp.float32
BF16 = jnp.bfloat16
VMEM_HEADROOM = 5 * 1024 * 1024


def _nbytes(shape, dtype):
    return int(np.prod(shape)) * jnp.dtype(dtype).itemsize


def _params(block_bytes, scratch_bytes=0, n_grid=2):
    limit = 2 * block_bytes + scratch_bytes + VMEM_HEADROOM
    return pltpu.CompilerParams(
        dimension_semantics=("arbitrary",) * n_grid,
        vmem_limit_bytes=int(limit),
    )


def _row_scale(ssq_ref):
    tot = jnp.sum(ssq_ref[...], axis=-1, keepdims=True)
    return lax.rsqrt(tot * (1.0 / D_MODEL) + RMS_EPS)


def _sigmoid(x):
    return 0.5 * jnp.tanh(0.5 * x) + 0.5


def _lane_partial_ssq(x):
    sq = x * x
    acc = sq[:, 0:LANES]
    for c in range(1, x.shape[1] // LANES):
        acc = acc + sq[:, c * LANES:(c + 1) * LANES]
    return acc


def _mm_call(kernel_fn, *, grid, in_specs, out_specs, out_shape, operands, block_bytes,
             temp_bytes, name, jobs=()):
    n_in, n_out = len(in_specs), len(out_specs)
    gj = grid[1]
    steps = grid[0] * grid[1]
    in_specs, out_specs, out_shape = list(in_specs), list(out_specs), list(out_shape)
    operands = list(operands)
    job_in_specs, job_operands, job_out_specs, job_out_shape, job_kind = [], [], [], [], []
    for src, layer, gain, rb, want_ssq in jobs:
        _, r, c = src.shape
        nbk = r // rb
        assert r % rb == 0 and nbk <= steps, (name, r, rb, steps)

        def blk(i, j, nbk=nbk):
            return jnp.minimum(i * gj + j, nbk - 1)

        job_in_specs.append(pl.BlockSpec((None, rb, c),
                                         lambda i, j, blk=blk, layer=layer: (layer, blk(i, j), 0)))
        job_operands.append(src)
        if gain is not None:
            job_in_specs.append(pl.BlockSpec((None, rb, 1),
                                             lambda i, j, blk=blk, layer=layer: (layer, blk(i, j), 0)))
            job_operands.append(gain)
        job_kind.append((gain is not None, want_ssq))
        job_out_specs.append(pl.BlockSpec((rb, c), lambda i, j, blk=blk: (blk(i, j), 0)))
        job_out_shape.append(jax.ShapeDtypeStruct((r, c), BF16))
        if want_ssq:
            job_out_specs.append(pl.BlockSpec((rb, LANES), lambda i, j, blk=blk: (blk(i, j), 0)))
            job_out_shape.append(jax.ShapeDtypeStruct((r, LANES), F32))
        block_bytes += (_nbytes((rb, c), F32) + _nbytes((rb, c), BF16)
                        + 2 * _nbytes((rb, LANES), F32))

    def body(*refs):
        main_in = refs[:n_in]
        pos = n_in
        job_in = []
        for hg, _ in job_kind:
            job_in.append((refs[pos], refs[pos + 1] if hg else None))
            pos += 2 if hg else 1
        main_out = refs[pos:pos + n_out]
        pos += n_out
        for (src_ref, gain_ref), (_, want_ssq) in zip(job_in, job_kind):
            w = src_ref[...]
            if gain_ref is not None:
                w = w * gain_ref[...]
            refs[pos][...] = w.astype(BF16)
            pos += 1
            if want_ssq:
                refs[pos][...] = _lane_partial_ssq(w)
                pos += 1
        kernel_fn(*main_in, *main_out)

    res = pl.pallas_call(
        body,
        grid=grid,
        in_specs=in_specs + job_in_specs,
        out_specs=out_specs + job_out_specs,
        out_shape=out_shape + job_out_shape,
        compiler_params=_params(block_bytes, temp_bytes),
        name=name,
    )(*operands, *job_operands)
    return tuple(res[:n_out]), tuple(res[n_out:])


def _prep_kernel(x_ref, xb_ref, ssq_ref):
    x = x_ref[...]
    xb_ref[...] = x.astype(BF16)
    ssq_ref[...] = _lane_partial_ssq(x)


def _prep(x):
    m = x.shape[0]
    tm = 512
    blocks = _nbytes((tm, D_MODEL), F32) + _nbytes((tm, D_MODEL), BF16) + _nbytes((tm, LANES), F32)
    return pl.pallas_call(
        _prep_kernel,
        grid=(m // tm,),
        in_specs=[pl.BlockSpec((tm, D_MODEL), lambda i: (i, 0))],
        out_specs=[pl.BlockSpec((tm, D_MODEL), lambda i: (i, 0)),
                   pl.BlockSpec((tm, LANES), lambda i: (i, 0))],
        out_shape=[jax.ShapeDtypeStruct((m, D_MODEL), BF16),
                   jax.ShapeDtypeStruct((m, LANES), F32)],
        compiler_params=_params(blocks, n_grid=1),
        name="prep",
    )(x)


def _mm_in_kernel(xb_ref, ssq_ref, w_ref, o_ref):
    acc = jnp.dot(xb_ref[...], w_ref[...], preferred_element_type=F32)
    o_ref[...] = (acc * _row_scale(ssq_ref)).astype(o_ref.dtype)


def _mm_in(xb, ssq, w, col0, ncols, out_dtype, name, tm=1024, tn=512, jobs=()):
    m, k = xb.shape
    s = ssq.shape[1]
    jb0 = col0 // tn
    blocks = (_nbytes((tm, k), BF16) + _nbytes((tm, s), F32) + _nbytes((k, tn), BF16)
              + _nbytes((tm, tn), out_dtype))
    (z,), cast = _mm_call(
        _mm_in_kernel,
        grid=(m // tm, ncols // tn),
        in_specs=[pl.BlockSpec((tm, k), lambda i, j: (i, 0)),
                  pl.BlockSpec((tm, s), lambda i, j: (i, 0)),
                  pl.BlockSpec((k, tn), lambda i, j: (0, j + jb0))],
        out_specs=[pl.BlockSpec((tm, tn), lambda i, j: (i, j))],
        out_shape=[jax.ShapeDtypeStruct((m, ncols), out_dtype)],
        operands=[xb, ssq, w],
        block_bytes=blocks, temp_bytes=_nbytes((tm, tn), F32), name=name, jobs=jobs)
    return z, cast


def _ret_kernel(ld_ref, q_ref, k_ref, v_ref, g_ref, gn_ref, o_ref,
                kvf_s, kvb_s, sf_s, sb_s, *, seq_len):
    c = RET_CHUNK
    n_chunks = seq_len // c
    grp = min(RET_GROUP, n_chunks)
    h = pl.program_id(1)
    lf = ld_ref[0, h]
    lb = ld_ref[1, h]
    scale = HEAD_DIM ** -0.5

    ii = lax.broadcasted_iota(jnp.int32, (c, c), 0)
    jj = lax.broadcasted_iota(jnp.int32, (c, c), 1)
    dist = (ii - jj).astype(F32)
    dmat = jnp.where(ii >= jj,
                     jnp.exp(lf * jnp.maximum(dist, 0.0)),
                     jnp.exp(lb * jnp.maximum(-dist, 0.0))) * scale
    pos = lax.broadcasted_iota(jnp.int32, (c, 1), 0).astype(F32)
    zeta_f = jnp.exp(lf * (c - 1.0 - pos)) * scale
    zeta_b = jnp.exp(lb * pos) * scale
    xi_f = jnp.exp(lf * (pos + 1.0))
    xi_b = jnp.exp(lb * (c - pos))
    one = jnp.ones((1, 1), F32)
    cd_f = jnp.exp(one * (lf * c))
    cd_b = jnp.exp(one * (lb * c))

    tdims = (((0,), (0,)), ((), ()))

    def rows(n):
        return pl.ds(pl.multiple_of(n * c, c), c)

    def summaries(n, carry):
        k = k_ref[rows(n), :]
        v = v_ref[rows(n), :].astype(F32)
        vz = jnp.concatenate([(v * zeta_f).astype(BF16), (v * zeta_b).astype(BF16)], axis=1)
        kv = lax.dot_general(k, vz, tdims, preferred_element_type=F32)
        kvf_s[n] = kv[:, :HEAD_DIM]
        kvb_s[n] = kv[:, HEAD_DIM:]
        return carry

    lax.fori_loop(0, n_chunks, summaries, 0, unroll=min(grp, 16))

    def scan(t, carry):
        st_f, st_b = carry
        nb = n_chunks - 1 - t
        sf_s[t] = st_f.astype(BF16)
        sb_s[nb] = st_b.astype(BF16)
        return st_f * cd_f + kvf_s[t], st_b * cd_b + kvb_s[nb]

    zero_state = jnp.zeros((HEAD_DIM, HEAD_DIM), F32)
    lax.fori_loop(0, n_chunks, scan, (zero_state, zero_state), unroll=2)

    gain = gn_ref[...]

    def outputs(ng, carry):
        scores, crosses = [], []
        for j in range(grp):
            n = ng * grp + j
            q = q_ref[rows(n), :]
            k = k_ref[rows(n), :]
            scores.append(lax.dot_general(q, k, (((1,), (1,)), ((), ())),
                                          preferred_element_type=F32))
            st = jnp.concatenate([sf_s[n], sb_s[n]], axis=1)
            crosses.append(jnp.dot(q, st, preferred_element_type=F32))
        probs = [(s * dmat).astype(BF16) for s in scores]
        for j in range(grp):
            n = ng * grp + j
            intra = jnp.dot(probs[j], v_ref[rows(n), :], preferred_element_type=F32)
            cross = crosses[j]
            y = intra + cross[:, :HEAD_DIM] * xi_f + cross[:, HEAD_DIM:] * xi_b
            mu = jnp.mean(y, axis=-1, keepdims=True)
            yc = y - mu
            var = jnp.mean(yc * yc, axis=-1, keepdims=True)
            yn = yc * lax.rsqrt(var + GN_EPS) * gain
            g = g_ref[rows(n), :].astype(F32)
            o_ref[rows(n), :] = (g * _sigmoid(g) * yn).astype(o_ref.dtype)
        return carry

    lax.fori_loop(0, n_chunks // grp, outputs, 0)


def _retention(z_ret, log_decay, gn_gain, batch, seq_len, n_heads=N_HEADS):
    m = batch * seq_len
    n_chunks = seq_len // RET_CHUNK
    assert n_chunks % min(RET_GROUP, n_chunks) == 0
    blk = (seq_len, HEAD_DIM)
    blocks = 5 * _nbytes(blk, BF16) + _nbytes((1, HEAD_DIM), F32)
    scratch = (2 * _nbytes((n_chunks, HEAD_DIM, HEAD_DIM), F32)
               + 2 * _nbytes((n_chunks, HEAD_DIM, HEAD_DIM), BF16)
               + min(RET_GROUP, n_chunks) * 3 * _nbytes((RET_CHUNK, 2 * HEAD_DIM), F32))

    def col(c):
        return lambda b, h: (b, h + c * n_heads)

    return pl.pallas_call(
        functools.partial(_ret_kernel, seq_len=seq_len),
        grid=(batch, n_heads),
        in_specs=[pl.BlockSpec(memory_space=pltpu.SMEM),
                  pl.BlockSpec(blk, col(0)), pl.BlockSpec(blk, col(1)),
                  pl.BlockSpec(blk, col(2)), pl.BlockSpec(blk, col(3)),
                  pl.BlockSpec((1, HEAD_DIM), lambda b, h: (0, h))],
        out_specs=pl.BlockSpec(blk, lambda b, h: (b, h)),
        out_shape=jax.ShapeDtypeStruct((m, n_heads * HEAD_DIM), BF16),
        scratch_shapes=[pltpu.VMEM((n_chunks, HEAD_DIM, HEAD_DIM), F32),
                        pltpu.VMEM((n_chunks, HEAD_DIM, HEAD_DIM), F32),
                        pltpu.VMEM((n_chunks, HEAD_DIM, HEAD_DIM), BF16),
                        pltpu.VMEM((n_chunks, HEAD_DIM, HEAD_DIM), BF16)],
        compiler_params=_params(blocks, scratch),
        name="retention",
    )(log_decay, z_ret, z_ret, z_ret, z_ret, gn_gain.reshape(1, -1))


def _attn_kernel(slope_ref, qraw_ref, kraw_ref, v_ref, qg_ref, kg_ref, o_ref,
                 q_ref, k_ref, q4, k4, v4, qd, kd, vd, bias_s, out_s, lse_s, *, seq_len):
    bq = ATT_BLOCK
    kw_max = bq + 2 * BRANCH_HALF
    h = pl.program_id(0)

    @pl.when(pl.program_id(1) == 0)
    def _():
        slope = slope_ref[h]
        row = lax.broadcasted_iota(jnp.int32, (bq, kw_max), 0)
        col = lax.broadcasted_iota(jnp.int32, (bq, kw_max), 1)
        for bi, d in enumerate(BRANCH_DILATIONS):
            kw = min(kw_max, seq_len // d)
            for var, off in enumerate((0, -BRANCH_HALF, bq - kw)):
                dist = jnp.abs(col - row + off)
                bias_s[bi, var] = jnp.where(dist <= BRANCH_HALF,
                                            dist.astype(F32) * (-slope * float(d)), NEG_BIG)

    q = qraw_ref[...]
    q_ref[...] = (q * lax.rsqrt(jnp.mean(q * q, axis=-1, keepdims=True) + RMS_EPS)
                  * qg_ref[...]) * (HEAD_DIM ** -0.5)
    k = kraw_ref[...]
    k_ref[...] = k * lax.rsqrt(jnp.mean(k * k, axis=-1, keepdims=True) + RMS_EPS) * kg_ref[...]

    m4 = seq_len // 4
    m16 = seq_len // 16
    for d in (4, 16, 1):
        bi = BRANCH_DILATIONS.index(d)
        m = seq_len // d
        nb = m // bq
        kw = min(kw_max, m)
        log_nb = nb.bit_length() - 1
        assert nb >= 1 and (1 << log_nb) == nb

        if d == 4:
            for r in range(4):
                dst = pl.ds(r * m4, m4)
                src = pl.ds(r, m4, stride=4)
                for src_ref, f32_ref, bf_ref in ((q_ref, q4, qd), (k_ref, k4, kd), (v_ref, v4, vd)):
                    val = src_ref[src, :]
                    f32_ref[dst, :] = val
                    bf_ref[dst, :] = val.astype(BF16)
        elif d == 16:
            for r4 in range(4):
                for r1 in range(4):
                    dst = pl.ds((r4 + 4 * r1) * m16, m16)
                    src = pl.ds(r4 * m4 + r1, m16, stride=4)
                    qd[dst, :] = q4[src, :].astype(BF16)
                    kd[dst, :] = k4[src, :].astype(BF16)
                    vd[dst, :] = v4[src, :].astype(BF16)
        else:
            qd[...] = q_ref[...].astype(BF16)
            kd[...] = k_ref[...].astype(BF16)
            vd[...] = v_ref[...].astype(BF16)

        bias_b = bias_s.at[bi]

        grp = _att_group(seq_len)

        def group(tg, carry, d=d, m=m, nb=nb, kw=kw, log_nb=log_nb, bias_b=bias_b, grp=grp):
            geo, scores, probs = [], [], []
            for g in range(grp):
                t = tg * grp + g
                tl = t & (nb - 1)
                q0 = tl * bq
                k0 = jnp.clip(q0 - BRANCH_HALF, 0, m - kw)
                kstart = pl.multiple_of((t - tl) * bq + k0, BRANCH_HALF)
                var = jnp.where(tl == 0, 0, jnp.where(tl == nb - 1, 2, 1))
                geo.append((t, q0, kstart, var))
                qb = qd[pl.ds(pl.multiple_of(t * bq, bq), bq), :]
                kb = kd[pl.ds(kstart, kw), :]
                scores.append(lax.dot_general(qb, kb, (((1,), (1,)), ((), ())),
                                              preferred_element_type=F32))
            for g in range(grp):
                s = scores[g] + bias_b[geo[g][3], :, 0:kw]
                mx = jnp.max(s, axis=-1, keepdims=True)
                e = jnp.exp(s - mx)
                den = jnp.sum(e, axis=-1, keepdims=True)
                probs.append((e.astype(BF16), mx, den))
            for g in range(grp):
                t, q0, kstart, _ = geo[g]
                e, mx, den = probs[g]
                acc = jnp.dot(e, vd[pl.ds(kstart, kw), :], preferred_element_type=F32)
                if d == 1:
                    rows = pl.ds(pl.multiple_of(t * bq, bq), bq)
                    l4, l16 = lse_s[0, rows, :], lse_s[1, rows, :]
                    shift = jnp.maximum(jnp.maximum(l4, l16), mx)
                    w1 = jnp.exp(mx - shift)
                    w4 = jnp.exp(l4 - shift)
                    w16 = jnp.exp(l16 - shift)
                    num = w1 * acc + w4 * out_s[0, rows, :] + w16 * out_s[1, rows, :]
                    o_ref[rows, :] = (num / (w1 * den + w4 + w16)).astype(o_ref.dtype)
                else:
                    slot = 0 if d == 4 else 1
                    dst = pl.ds((t >> log_nb) + d * q0, bq, stride=d)
                    out_s[slot, dst, :] = acc / den
                    lse_s[slot, dst, :] = jnp.broadcast_to(mx + jnp.log(den), (bq, HEAD_DIM))
            return carry

        lax.fori_loop(0, seq_len // (bq * grp), group, 0)


def _att_group(seq_len):
    return max(ATT_GROUP_MIN, min(ATT_GROUP_MAX, seq_len // (2 * ATT_BLOCK)))


def _alibi_slopes(n):
    return jnp.asarray(2.0 ** (-8.0 * np.arange(1, n + 1) / n), F32)


def _attention(z_att, q_gain, k_gain, batch, seq_len, n_heads=N_HEADS):
    m = batch * seq_len
    assert (seq_len // ATT_BLOCK) % _att_group(seq_len) == 0
    assert seq_len // max(BRANCH_DILATIONS) >= ATT_BLOCK
    blk = (seq_len, HEAD_DIM)
    bias_shape = (len(BRANCH_DILATIONS), 3, ATT_BLOCK, ATT_BLOCK + 2 * BRANCH_HALF)
    blocks = 3 * _nbytes(blk, F32) + _nbytes(blk, BF16) + 2 * _nbytes((1, HEAD_DIM), F32)
    scratch = (5 * _nbytes(blk, F32) + 3 * _nbytes(blk, BF16) + _nbytes(bias_shape, F32)
               + 2 * _nbytes((2,) + blk, F32))

    def col(c):
        return lambda h, b: (b, h + c * n_heads)

    return pl.pallas_call(
        functools.partial(_attn_kernel, seq_len=seq_len),
        grid=(n_heads, batch),
        in_specs=[pl.BlockSpec(memory_space=pltpu.SMEM),
                  pl.BlockSpec(blk, col(0)), pl.BlockSpec(blk, col(1)), pl.BlockSpec(blk, col(2)),
                  pl.BlockSpec((1, HEAD_DIM), lambda h, b: (0, 0)),
                  pl.BlockSpec((1, HEAD_DIM), lambda h, b: (0, 0))],
        out_specs=pl.BlockSpec(blk, lambda h, b: (b, h)),
        out_shape=jax.ShapeDtypeStruct((m, n_heads * HEAD_DIM), BF16),
        scratch_shapes=[pltpu.VMEM(blk, F32)] * 5
                       + [pltpu.VMEM(blk, BF16)] * 3
                       + [pltpu.VMEM(bias_shape, F32),
                          pltpu.VMEM((2,) + blk, F32), pltpu.VMEM((2,) + blk, F32)],
        compiler_params=_params(blocks, scratch),
        name="dilated_attention",
    )(_alibi_slopes(n_heads), z_att, z_att, z_att,
      q_gain.reshape(1, -1), k_gain.reshape(1, -1))


def _residual_outputs(x_new, x_ref, xb_ref, ssq_ref):
    x_ref[...] = x_new
    xb_ref[...] = x_new.astype(BF16)
    part = _lane_partial_ssq(x_new)
    j = pl.program_id(1)

    @pl.when(j == 0)
    def _():
        ssq_ref[...] = part

    @pl.when(j > 0)
    def _():
        ssq_ref[...] += part


def _residual_specs(m, tm, tn):
    specs = [pl.BlockSpec((tm, tn), lambda i, j: (i, j)),
             pl.BlockSpec((tm, tn), lambda i, j: (i, j)),
             pl.BlockSpec((tm, LANES), lambda i, j: (i, 0))]
    shapes = [jax.ShapeDtypeStruct((m, D_MODEL), F32),
              jax.ShapeDtypeStruct((m, D_MODEL), BF16),
              jax.ShapeDtypeStruct((m, LANES), F32)]
    nbytes = _nbytes((tm, tn), F32) + _nbytes((tm, tn), BF16) + _nbytes((tm, LANES), F32)
    return specs, shapes, nbytes


def _mm_out_kernel(ret_ref, att_ref, wt_ref, wb_ref, x_ref, xo_ref, xb_ref, ssq_ref):
    acc = jnp.dot(ret_ref[...], wt_ref[...], preferred_element_type=F32)
    acc = acc + jnp.dot(att_ref[...], wb_ref[...], preferred_element_type=F32)
    _residual_outputs(x_ref[...] + acc, xo_ref, xb_ref, ssq_ref)


def _mm_out(ret, att, w, x, tm=1024, tn=512, jobs=()):
    m = x.shape[0]
    kh = MIX_HALF
    out_specs, out_shape, out_bytes = _residual_specs(m, tm, tn)
    blocks = (2 * _nbytes((tm, kh), BF16) + 2 * _nbytes((kh, tn), BF16) + _nbytes((tm, tn), F32)
              + out_bytes)
    return _mm_call(
        _mm_out_kernel,
        grid=(m // tm, D_MODEL // tn),
        in_specs=[pl.BlockSpec((tm, kh), lambda i, j: (i, 0)),
                  pl.BlockSpec((tm, kh), lambda i, j: (i, 0)),
                  pl.BlockSpec((kh, tn), lambda i, j: (0, j)),
                  pl.BlockSpec((kh, tn), lambda i, j: (1, j)),
                  pl.BlockSpec((tm, tn), lambda i, j: (i, j))],
        out_specs=out_specs, out_shape=out_shape,
        operands=[ret, att, w, w, x],
        block_bytes=blocks, temp_bytes=2 * _nbytes((tm, tn), F32), name="mm_out", jobs=jobs)


def _mm_gateup_kernel(xb_ref, ssq_ref, wg_ref, wu_ref, h_ref):
    r = _row_scale(ssq_ref)
    xb = xb_ref[...]
    a = jnp.dot(xb, wg_ref[...], preferred_element_type=F32) * r
    u = jnp.dot(xb, wu_ref[...], preferred_element_type=F32) * r
    h_ref[...] = (a * _sigmoid(a) * u).astype(h_ref.dtype)


def _mm_gateup(xb, ssq, wg, wu, col0, ncols, tn, name, tm=1024, jobs=()):
    m, k = xb.shape
    s = ssq.shape[1]
    jb0 = col0 // tn
    blocks = (_nbytes((tm, k), BF16) + _nbytes((tm, s), F32) + 2 * _nbytes((k, tn), BF16)
              + _nbytes((tm, tn), BF16))
    (hid,), cast = _mm_call(
        _mm_gateup_kernel,
        grid=(m // tm, ncols // tn),
        in_specs=[pl.BlockSpec((tm, k), lambda i, j: (i, 0)),
                  pl.BlockSpec((tm, s), lambda i, j: (i, 0)),
                  pl.BlockSpec((k, tn), lambda i, j: (0, j + jb0)),
                  pl.BlockSpec((k, tn), lambda i, j: (0, j + jb0))],
        out_specs=[pl.BlockSpec((tm, tn), lambda i, j: (i, j))],
        out_shape=[jax.ShapeDtypeStruct((m, ncols), BF16)],
        operands=[xb, ssq, wg, wu],
        block_bytes=blocks, temp_bytes=3 * _nbytes((tm, tn), F32), name=name, jobs=jobs)
    return hid, cast


def _ff_split(d_ff):
    main = d_ff // FF_TILE * FF_TILE
    return main, d_ff - main


def _mm_down_kernel(hm_ref, ht_ref, wm_ref, wt_ref, x_ref, xo_ref, xb_ref, ssq_ref):
    acc = jnp.dot(hm_ref[...], wm_ref[...], preferred_element_type=F32)
    acc = acc + jnp.dot(ht_ref[...], wt_ref[...], preferred_element_type=F32)
    _residual_outputs(x_ref[...] + acc, xo_ref, xb_ref, ssq_ref)


def _mm_down(hid_main, hid_tail, w, x, tm=512, tn=512, jobs=()):
    m, k_main = hid_main.shape
    k_tail = hid_tail.shape[1]
    assert k_main % k_tail == 0
    tail_blk = k_main // k_tail
    out_specs, out_shape, out_bytes = _residual_specs(m, tm, tn)
    blocks = (_nbytes((tm, k_main + k_tail), BF16) + _nbytes((k_main + k_tail, tn), BF16)
              + _nbytes((tm, tn), F32) + out_bytes)
    return _mm_call(
        _mm_down_kernel,
        grid=(m // tm, D_MODEL // tn),
        in_specs=[pl.BlockSpec((tm, k_main), lambda i, j: (i, 0)),
                  pl.BlockSpec((tm, k_tail), lambda i, j: (i, 0)),
                  pl.BlockSpec((k_main, tn), lambda i, j: (0, j)),
                  pl.BlockSpec((k_tail, tn), lambda i, j: (tail_blk, j)),
                  pl.BlockSpec((tm, tn), lambda i, j: (i, j))],
        out_specs=out_specs, out_shape=out_shape,
        operands=[hid_main, hid_tail, w, w, x],
        block_bytes=blocks, temp_bytes=2 * _nbytes((tm, tn), F32), name="mm_down", jobs=jobs)


def _mm_ple_kernel(xb_ref, ssq_ref, wg_ref, p_ref, wp_ref, x_ref, xo_ref, xb_out_ref, ssq_out_ref):
    e = jnp.dot(xb_ref[...], wg_ref[...], preferred_element_type=F32) * _row_scale(ssq_ref)
    proj = jnp.dot(p_ref[...].astype(BF16), wp_ref[...], preferred_element_type=F32)
    _residual_outputs(x_ref[...] + _sigmoid(e) * proj, xo_ref, xb_out_ref, ssq_out_ref)


def _mm_ple(xb, ssq, wg, p, wp, x, tm=1024, tn=512, jobs=()):
    m, k = xb.shape
    s = ssq.shape[1]
    out_specs, out_shape, out_bytes = _residual_specs(m, tm, tn)
    blocks = (_nbytes((tm, k), BF16) + _nbytes((tm, s), F32) + _nbytes((k, tn), BF16)
              + _nbytes((tm, PLE_DIM), F32) + _nbytes((PLE_DIM, tn), BF16)
              + _nbytes((tm, tn), F32) + out_bytes)
    return _mm_call(
        _mm_ple_kernel,
        grid=(m // tm, D_MODEL // tn),
        in_specs=[pl.BlockSpec((tm, k), lambda i, j: (i, 0)),
                  pl.BlockSpec((tm, s), lambda i, j: (i, 0)),
                  pl.BlockSpec((k, tn), lambda i, j: (0, j)),
                  pl.BlockSpec((tm, PLE_DIM), lambda i, j: (i, 0)),
                  pl.BlockSpec((PLE_DIM, tn), lambda i, j: (0, j)),
                  pl.BlockSpec((tm, tn), lambda i, j: (i, j))],
        out_specs=out_specs, out_shape=out_shape,
        operands=[xb, ssq, wg, p, wp, x],
        block_bytes=blocks, temp_bytes=2 * _nbytes((tm, tn), F32), name="mm_ple", jobs=jobs)


_HOSTED = {
    (0, "mm_in_ret"): [("w_out", 0), ("w_gate", 0)],
    (0, "mm_out"): [("w_up", 0), ("w_ple_gate", 0)],
    (0, "mm_gateup"): [("w_down", 0), ("w_in", 1)],
    (0, "mm_down"): [("w_out", 1)],
    (0, "mm_ple"): [("w_up", 1)],
    (1, "mm_in_ret"): [("w_gate", 1)],
    (1, "mm_gateup"): [("w_down", 1), ("w_ple_gate", 1), ("x_next", 0)],
}
_NEXT_X = "x_next"
_GAIN_OF = {"w_in": "ln_mix", "w_gate": "ln_ffn", "w_up": "ln_ffn", "w_ple_gate": "ln_ple"}
_ROW_BLOCK = {"w_in": 32, "w_out": 32, "w_gate": 64, "w_up": 64, "w_down": 128, "w_ple_gate": 32,
              "x_next": 64}
_ROW_BLOCK_OVERRIDE = {(0, "mm_in_ret", "w_out"): 64, (0, "mm_out", "w_ple_gate"): 64}


def _run_trunk(x, p, weights, raw, host, prepped=None):
    batch, seq_len, _ = x.shape
    m = batch * seq_len
    x = x.reshape(m, D_MODEL)
    xb, ssq = _prep(x) if prepped is None else prepped
    next_prep = []
    ff_main, ff_tail = _ff_split(raw["w_gate"].shape[2])
    assert ff_tail > 0 and ff_tail % LANES == 0

    def jobs_for(layer, call):
        if not host:
            return [], []
        names = _HOSTED.get((layer, call), [])
        jobs = []
        for name, wl in names:
            gain = raw[_GAIN_OF[name]][..., None] if name in _GAIN_OF else None
            rb = _ROW_BLOCK_OVERRIDE.get((layer, call, name), _ROW_BLOCK[name])
            jobs.append((raw[name], wl, gain, rb, name == _NEXT_X))
        return names, jobs

    def store(names, cast):
        cast = list(cast)
        for name, wl in names:
            if name == _NEXT_X:
                next_prep.extend([cast.pop(0), cast.pop(0)])
            else:
                weights[wl][name] = cast.pop(0)

    for i in range(len(weights)):
        lw = weights[i]
        names, jobs = jobs_for(i, "mm_in_ret")
        z_ret, cast = _mm_in(xb, ssq, lw["w_in"], 0, 4 * MIX_HALF, BF16, "mm_in_ret", tn=1024,
                             jobs=jobs)
        store(names, cast)
        names, jobs = jobs_for(i, "mm_in_att")
        z_att, cast = _mm_in(xb, ssq, lw["w_in"], 4 * MIX_HALF, 3 * MIX_HALF, F32, "mm_in_att",
                             tn=512 if jobs else 1024, jobs=jobs)
        store(names, cast)
        ret = _retention(z_ret, raw["ret_log_decay"][i], raw["ret_gn"][i], batch, seq_len)
        att = _attention(z_att, raw["q_norm"][i], raw["k_norm"][i], batch, seq_len)
        names, jobs = jobs_for(i, "mm_out")
        (x, xb, ssq), cast = _mm_out(ret, att, lw["w_out"], x, jobs=jobs)
        store(names, cast)
        names, jobs = jobs_for(i, "mm_gateup")
        hid_main, cast = _mm_gateup(xb, ssq, lw["w_gate"], lw["w_up"], 0, ff_main, FF_TILE,
                                    "mm_gateup", jobs=jobs)
        store(names, cast)
        hid_tail, _ = _mm_gateup(xb, ssq, lw["w_gate"], lw["w_up"], ff_main, ff_tail, ff_tail,
                                 "mm_gateup_tail")
        names, jobs = jobs_for(i, "mm_down")
        (x, xb, ssq), cast = _mm_down(hid_main, hid_tail, lw["w_down"], x, jobs=jobs)
        store(names, cast)
        names, jobs = jobs_for(i, "mm_ple")
        (x, xb, ssq), cast = _mm_ple(xb, ssq, lw["w_ple_gate"], p[i].reshape(m, PLE_DIM),
                                     lw["w_ple_proj"], x, jobs=jobs)
        store(names, cast)
    return x.reshape(batch, seq_len, D_MODEL), tuple(next_prep) or None


def kernel(x_prompt, x_sample, p_prompt, p_sample, ln_mix, w_in, ret_log_decay, ret_gn, q_norm,
           k_norm, w_out, ln_ffn, w_gate, w_up, w_down, ln_ple, w_ple_gate, w_ple_proj):
    depth = w_in.shape[0]
    raw = dict(ln_mix=ln_mix, w_in=w_in, ret_log_decay=ret_log_decay, ret_gn=ret_gn, q_norm=q_norm,
               k_norm=k_norm, w_out=w_out, ln_ffn=ln_ffn, w_gate=w_gate, w_up=w_up, w_down=w_down,
               ln_ple=ln_ple, w_ple_gate=w_ple_gate, w_ple_proj=w_ple_proj,
               x_next=x_sample.reshape(1, -1, D_MODEL))
    hosted = {(name, wl) for names in _HOSTED.values() for name, wl in names} if depth == 2 else set()
    weights = []
    for i in range(depth):
        lw = {"w_ple_proj": w_ple_proj[i].astype(BF16)}
        for name in ("w_in", "w_out", "w_gate", "w_up", "w_down", "w_ple_gate"):
            if (name, i) in hosted:
                continue
            w = raw[name][i]
            if name in _GAIN_OF:
                w = raw[_GAIN_OF[name]][i][:, None] * w
            lw[name] = w.astype(BF16)
        weights.append(lw)
    y_prompt, sample_prep = _run_trunk(x_prompt, p_prompt, weights, raw, host=depth == 2)
    y_sample, _ = _run_trunk(x_sample, p_sample, weights, raw, host=False, prepped=sample_prep)
    return (y_prompt, y_sample)
```

```python
import functools

import jax
import jax.numpy as jnp
import numpy as np
from jax import lax
from jax.experimental import pallas as pl
from jax.experimental.pallas import tpu as pltpu

D_MODEL = 4096
HEAD_DIM = 128
N_HEADS = 16
MIX_HALF = N_HEADS * HEAD_DIM
PLE_DIM = 256
RMS_EPS = 1e-6
GN_EPS = 1e-5
NEG_BIG = -1e30
RET_CHUNK = 128
RET_GROUP = 32
BRANCH_DILATIONS = (1, 4, 16)
BRANCH_HALF = 64
LANES = 128
ATT_BLOCK = 128
ATT_GROUP_MIN, ATT_GROUP_MAX = 8, 16
DEEP_BUFFERS = 3
FF_TILE = 512

F32 = jnp.float32
BF16 = jnp.bfloat16
VMEM_HEADROOM = 5 * 1024 * 1024


def _nbytes(shape, dtype):
    return int(np.prod(shape)) * jnp.dtype(dtype).itemsize


def _params(block_bytes, scratch_bytes=0, n_grid=2):
    limit = 2 * block_bytes + scratch_bytes + VMEM_HEADROOM
    return pltpu.CompilerParams(
        dimension_semantics=("arbitrary",) * n_grid,
        vmem_limit_bytes=int(limit),
    )


def _row_scale(ssq_ref):
    tot = jnp.sum(ssq_ref[...], axis=-1, keepdims=True)
    return lax.rsqrt(tot * (1.0 / D_MODEL) + RMS_EPS)


def _sigmoid(x):
    return 0.5 * jnp.tanh(0.5 * x) + 0.5


def _lane_partial_ssq(x):
    sq = x * x
    acc = sq[:, 0:LANES]
    for c in range(1, x.shape[1] // LANES):
        acc = acc + sq[:, c * LANES:(c + 1) * LANES]
    return acc


def _mm_call(kernel_fn, *, grid, in_specs, out_specs, out_shape, operands, block_bytes,
             temp_bytes, name, jobs=(), deep=()):
    n_in, n_out = len(in_specs), len(out_specs)
    gj = grid[1]
    steps = grid[0] * grid[1]
    in_specs, out_specs, out_shape = list(in_specs), list(out_specs), list(out_shape)
    operands = list(operands)
    job_in_specs, job_operands, job_out_specs, job_out_shape, job_kind = [], [], [], [], []
    for src, layer, gain, rb, want_ssq in jobs:
        _, r, c = src.shape
        nbk = r // rb
        assert r % rb == 0 and nbk <= steps, (name, r, rb, steps)

        def blk(i, j, nbk=nbk):
            return jnp.minimum(i * gj + j, nbk - 1)

        job_in_specs.append(pl.BlockSpec((None, rb, c),
                                         lambda i, j, blk=blk, layer=layer: (layer, blk(i, j), 0)))
        job_operands.append(src)
        if gain is not None:
            job_in_specs.append(pl.BlockSpec((None, rb, 1),
                                             lambda i, j, blk=blk, layer=layer: (layer, blk(i, j), 0)))
            job_operands.append(gain)
        job_kind.append((gain is not None, want_ssq))
        job_out_specs.append(pl.BlockSpec((rb, c), lambda i, j, blk=blk: (blk(i, j), 0)))
        job_out_shape.append(jax.ShapeDtypeStruct((r, c), BF16))
        if want_ssq:
            job_out_specs.append(pl.BlockSpec((rb, LANES), lambda i, j, blk=blk: (blk(i, j), 0)))
            job_out_shape.append(jax.ShapeDtypeStruct((r, LANES), F32))
        block_bytes += (_nbytes((rb, c), F32) + _nbytes((rb, c), BF16)
                        + 2 * _nbytes((rb, LANES), F32))

    def body(*refs):
        main_in = refs[:n_in]
        pos = n_in
        job_in = []
        for hg, _ in job_kind:
            job_in.append((refs[pos], refs[pos + 1] if hg else None))
            pos += 2 if hg else 1
        main_out = refs[pos:pos + n_out]
        pos += n_out
        for (src_ref, gain_ref), (_, want_ssq) in zip(job_in, job_kind):
            w = src_ref[...]
            if gain_ref is not None:
                w = w * gain_ref[...]
            refs[pos][...] = w.astype(BF16)
            pos += 1
            if want_ssq:
                refs[pos][...] = _lane_partial_ssq(w)
                pos += 1
        kernel_fn(*main_in, *main_out)

    if deep and not jobs:
        extra = 0
        for d in deep:
            spec = in_specs[d]
            in_specs[d] = pl.BlockSpec(spec.block_shape, spec.index_map,
                                       pipeline_mode=pl.Buffered(DEEP_BUFFERS))
            extra += (DEEP_BUFFERS - 2) * _nbytes(spec.block_shape, operands[d].dtype)
        inner = pltpu.emit_pipeline(body, grid=grid, in_specs=in_specs, out_specs=out_specs)
        any_spec = pl.BlockSpec(memory_space=pl.ANY)
        res = pl.pallas_call(
            lambda *refs: inner(*refs),
            in_specs=[any_spec] * n_in,
            out_specs=[any_spec] * n_out,
            out_shape=out_shape,
            compiler_params=pltpu.CompilerParams(
                vmem_limit_bytes=int(2 * block_bytes + extra + temp_bytes + VMEM_HEADROOM)),
            name=name,
        )(*operands)
        return tuple(res), ()

    res = pl.pallas_call(
        body,
        grid=grid,
        in_specs=in_specs + job_in_specs,
        out_specs=out_specs + job_out_specs,
        out_shape=out_shape + job_out_shape,
        compiler_params=_params(block_bytes, temp_bytes),
        name=name,
    )(*operands, *job_operands)
    return tuple(res[:n_out]), tuple(res[n_out:])


def _prep_kernel(x_ref, xb_ref, ssq_ref):
    x = x_ref[...]
    xb_ref[...] = x.astype(BF16)
    ssq_ref[...] = _lane_partial_ssq(x)


def _prep(x):
    m = x.shape[0]
    tm = 512
    blocks = _nbytes((tm, D_MODEL), F32) + _nbytes((tm, D_MODEL), BF16) + _nbytes((tm, LANES), F32)
    return pl.pallas_call(
        _prep_kernel,
        grid=(m // tm,),
        in_specs=[pl.BlockSpec((tm, D_MODEL), lambda i: (i, 0))],
        out_specs=[pl.BlockSpec((tm, D_MODEL), lambda i: (i, 0)),
                   pl.BlockSpec((tm, LANES), lambda i: (i, 0))],
        out_shape=[jax.ShapeDtypeStruct((m, D_MODEL), BF16),
                   jax.ShapeDtypeStruct((m, LANES), F32)],
        compiler_params=_params(blocks, n_grid=1),
        name="prep",
    )(x)


def _mm_in_kernel(xb_ref, ssq_ref, w_ref, o_ref):
    acc = jnp.dot(xb_ref[...], w_ref[...], preferred_element_type=F32)
    o_ref[...] = (acc * _row_scale(ssq_ref)).astype(o_ref.dtype)


def _mm_in(xb, ssq, w, col0, ncols, out_dtype, name, tm=1024, tn=512, jobs=()):
    m, k = xb.shape
    s = ssq.shape[1]
    jb0 = col0 // tn
    blocks = (_nbytes((tm, k), BF16) + _nbytes((tm, s), F32) + _nbytes((k, tn), BF16)
              + _nbytes((tm, tn), out_dtype))
    (z,), cast = _mm_call(
        _mm_in_kernel,
        grid=(m // tm, ncols // tn),
        in_specs=[pl.BlockSpec((tm, k), lambda i, j: (i, 0)),
                  pl.BlockSpec((tm, s), lambda i, j: (i, 0)),
                  pl.BlockSpec((k, tn), lambda i, j: (0, j + jb0))],
        out_specs=[pl.BlockSpec((tm, tn), lambda i, j: (i, j))],
        out_shape=[jax.ShapeDtypeStruct((m, ncols), out_dtype)],
        operands=[xb, ssq, w],
        block_bytes=blocks, temp_bytes=_nbytes((tm, tn), F32), name=name, jobs=jobs)
    return z, cast


def _ret_kernel(ld_ref, q_ref, k_ref, v_ref, g_ref, gn_ref, o_ref,
                kvf_s, kvb_s, sf_s, sb_s, *, seq_len):
    c = RET_CHUNK
    n_chunks = seq_len // c
    grp = min(RET_GROUP, n_chunks)
    h = pl.program_id(1)
    lf = ld_ref[0, h]
    lb = ld_ref[1, h]
    scale = HEAD_DIM ** -0.5

    ii = lax.broadcasted_iota(jnp.int32, (c, c), 0)
    jj = lax.broadcasted_iota(jnp.int32, (c, c), 1)
    dist = (ii - jj).astype(F32)
    dmat = jnp.where(ii >= jj,
                     jnp.exp(lf * jnp.maximum(dist, 0.0)),
                     jnp.exp(lb * jnp.maximum(-dist, 0.0))) * scale
    pos = lax.broadcasted_iota(jnp.int32, (c, 1), 0).astype(F32)
    zeta_f = jnp.exp(lf * (c - 1.0 - pos)) * scale
    zeta_b = jnp.exp(lb * pos) * scale
    xi_f = jnp.exp(lf * (pos + 1.0))
    xi_b = jnp.exp(lb * (c - pos))
    one = jnp.ones((1, 1), F32)
    cd_f = jnp.exp(one * (lf * c))
    cd_b = jnp.exp(one * (lb * c))

    tdims = (((0,), (0,)), ((), ()))

    def rows(n):
        return pl.ds(pl.multiple_of(n * c, c), c)

    def summaries(n, carry):
        k = k_ref[rows(n), :]
        v = v_ref[rows(n), :].astype(F32)
        vz = jnp.concatenate([(v * zeta_f).astype(BF16), (v * zeta_b).astype(BF16)], axis=1)
        kv = lax.dot_general(k, vz, tdims, preferred_element_type=F32)
        kvf_s[n] = kv[:, :HEAD_DIM]
        kvb_s[n] = kv[:, HEAD_DIM:]
        return carry

    lax.fori_loop(0, n_chunks, summaries, 0, unroll=min(grp, 16))

    def scan(t, carry):
        st_f, st_b = carry
        nb = n_chunks - 1 - t
        sf_s[t] = st_f.astype(BF16)
        sb_s[nb] = st_b.astype(BF16)
        return st_f * cd_f + kvf_s[t], st_b * cd_b + kvb_s[nb]

    zero_state = jnp.zeros((HEAD_DIM, HEAD_DIM), F32)
    lax.fori_loop(0, n_chunks, scan, (zero_state, zero_state), unroll=2)

    gain = gn_ref[...]

    def outputs(ng, carry):
        scores, crosses = [], []
        for j in range(grp):
            n = ng * grp + j
            q = q_ref[rows(n), :]
            k = k_ref[rows(n), :]
            scores.append(lax.dot_general(q, k, (((1,), (1,)), ((), ())),
                                          preferred_element_type=F32))
            st = jnp.concatenate([sf_s[n], sb_s[n]], axis=1)
            crosses.append(jnp.dot(q, st, preferred_element_type=F32))
        probs = [(s * dmat).astype(BF16) for s in scores]
        for j in range(grp):
            n = ng * grp + j
            intra = jnp.dot(probs[j], v_ref[rows(n), :], preferred_element_type=F32)
            cross = crosses[j]
            y = intra + cross[:, :HEAD_DIM] * xi_f + cross[:, HEAD_DIM:] * xi_b
            mu = jnp.mean(y, axis=-1, keepdims=True)
            yc = y - mu
            var = jnp.mean(yc * yc, axis=-1, keepdims=True)
            yn = yc * lax.rsqrt(var + GN_EPS) * gain
            g = g_ref[rows(n), :].astype(F32)
            o_ref[rows(n), :] = (g * _sigmoid(g) * yn).astype(o_ref.dtype)
        return carry

    lax.fori_loop(0, n_chunks // grp, outputs, 0)


def _retention(z_ret, log_decay, gn_gain, batch, seq_len, n_heads=N_HEADS):
    m = batch * seq_len
    n_chunks = seq_len // RET_CHUNK
    assert n_chunks % min(RET_GROUP, n_chunks) == 0
    blk = (seq_len, HEAD_DIM)
    blocks = 5 * _nbytes(blk, BF16) + _nbytes((1, HEAD_DIM), F32)
    scratch = (2 * _nbytes((n_chunks, HEAD_DIM, HEAD_DIM), F32)
               + 2 * _nbytes((n_chunks, HEAD_DIM, HEAD_DIM), BF16)
               + min(RET_GROUP, n_chunks) * 3 * _nbytes((RET_CHUNK, 2 * HEAD_DIM), F32))

    def col(c):
        return lambda b, h: (b, h + c * n_heads)

    return pl.pallas_call(
        functools.partial(_ret_kernel, seq_len=seq_len),
        grid=(batch, n_heads),
        in_specs=[pl.BlockSpec(memory_space=pltpu.SMEM),
                  pl.BlockSpec(blk, col(0)), pl.BlockSpec(blk, col(1)),
                  pl.BlockSpec(blk, col(2)), pl.BlockSpec(blk, col(3)),
                  pl.BlockSpec((1, HEAD_DIM), lambda b, h: (0, h))],
        out_specs=pl.BlockSpec(blk, lambda b, h: (b, h)),
        out_shape=jax.ShapeDtypeStruct((m, n_heads * HEAD_DIM), BF16),
        scratch_shapes=[pltpu.VMEM((n_chunks, HEAD_DIM, HEAD_DIM), F32),
                        pltpu.VMEM((n_chunks, HEAD_DIM, HEAD_DIM), F32),
                        pltpu.VMEM((n_chunks, HEAD_DIM, HEAD_DIM), BF16),
                        pltpu.VMEM((n_chunks, HEAD_DIM, HEAD_DIM), BF16)],
        compiler_params=_params(blocks, scratch),
        name="retention",
    )(log_decay, z_ret, z_ret, z_ret, z_ret, gn_gain.reshape(1, -1))


def _attn_kernel(slope_ref, qraw_ref, kraw_ref, v_ref, qg_ref, kg_ref, o_ref,
                 q_ref, k_ref, q4, k4, v4, qd, kd, vd, bias_s, out_s, lse_s, *, seq_len):
    bq = ATT_BLOCK
    kw_max = bq + 2 * BRANCH_HALF
    h = pl.program_id(0)

    @pl.when(pl.program_id(1) == 0)
    def _():
        slope = slope_ref[h]
        row = lax.broadcasted_iota(jnp.int32, (bq, kw_max), 0)
        col = lax.broadcasted_iota(jnp.int32, (bq, kw_max), 1)
        for bi, d in enumerate(BRANCH_DILATIONS):
            kw = min(kw_max, seq_len // d)
            for var, off in enumerate((0, -BRANCH_HALF, bq - kw)):
                dist = jnp.abs(col - row + off)
                bias_s[bi, var] = jnp.where(dist <= BRANCH_HALF,
                                            dist.astype(F32) * (-slope * float(d)), NEG_BIG)

    q = qraw_ref[...]
    q_ref[...] = (q * lax.rsqrt(jnp.mean(q * q, axis=-1, keepdims=True) + RMS_EPS)
                  * qg_ref[...]) * (HEAD_DIM ** -0.5)
    k = kraw_ref[...]
    k_ref[...] = k * lax.rsqrt(jnp.mean(k * k, axis=-1, keepdims=True) + RMS_EPS) * kg_ref[...]

    m4 = seq_len // 4
    m16 = seq_len // 16
    for d in (4, 16, 1):
        bi = BRANCH_DILATIONS.index(d)
        m = seq_len // d
        nb = m // bq
        kw = min(kw_max, m)
        log_nb = nb.bit_length() - 1
        assert nb >= 1 and (1 << log_nb) == nb

        if d == 4:
            for r in range(4):
                dst = pl.ds(r * m4, m4)
                src = pl.ds(r, m4, stride=4)
                for src_ref, f32_ref, bf_ref in ((q_ref, q4, qd), (k_ref, k4, kd), (v_ref, v4, vd)):
                    val = src_ref[src, :]
                    f32_ref[dst, :] = val
                    bf_ref[dst, :] = val.astype(BF16)
        elif d == 16:
            for r4 in range(4):
                for r1 in range(4):
                    dst = pl.ds((r4 + 4 * r1) * m16, m16)
                    src = pl.ds(r4 * m4 + r1, m16, stride=4)
                    qd[dst, :] = q4[src, :].astype(BF16)
                    kd[dst, :] = k4[src, :].astype(BF16)
                    vd[dst, :] = v4[src, :].astype(BF16)
        else:
            qd[...] = q_ref[...].astype(BF16)
            kd[...] = k_ref[...].astype(BF16)
            vd[...] = v_ref[...].astype(BF16)

        bias_b = bias_s.at[bi]

        grp = _att_group(seq_len)

        def group(tg, carry, d=d, m=m, nb=nb, kw=kw, log_nb=log_nb, bias_b=bias_b, grp=grp):
            geo, scores, probs = [], [], []
            for g in range(grp):
                t = tg * grp + g
                tl = t & (nb - 1)
                q0 = tl * bq
                k0 = jnp.clip(q0 - BRANCH_HALF, 0, m - kw)
                kstart = pl.multiple_of((t - tl) * bq + k0, BRANCH_HALF)
                var = jnp.where(tl == 0, 0, jnp.where(tl == nb - 1, 2, 1))
                geo.append((t, q0, kstart, var))
                qb = qd[pl.ds(pl.multiple_of(t * bq, bq), bq), :]
                kb = kd[pl.ds(kstart, kw), :]
                scores.append(lax.dot_general(qb, kb, (((1,), (1,)), ((), ())),
                                              preferred_element_type=F32))
            for g in range(grp):
                s = scores[g] + bias_b[geo[g][3], :, 0:kw]
                mx = jnp.max(s, axis=-1, keepdims=True)
                e = jnp.exp(s - mx)
                den = jnp.sum(e, axis=-1, keepdims=True)
                probs.append((e.astype(BF16), mx, den))
            for g in range(grp):
                t, q0, kstart, _ = geo[g]
                e, mx, den = probs[g]
                acc = jnp.dot(e, vd[pl.ds(kstart, kw), :], preferred_element_type=F32)
                if d == 1:
                    rows = pl.ds(pl.multiple_of(t * bq, bq), bq)
                    l4, l16 = lse_s[0, rows, :], lse_s[1, rows, :]
                    shift = jnp.maximum(jnp.maximum(l4, l16), mx)
                    w1 = jnp.exp(mx - shift)
                    w4 = jnp.exp(l4 - shift)
                    w16 = jnp.exp(l16 - shift)
                    num = w1 * acc + w4 * out_s[0, rows, :] + w16 * out_s[1, rows, :]
                    o_ref[rows, :] = (num / (w1 * den + w4 + w16)).astype(o_ref.dtype)
                else:
                    slot = 0 if d == 4 else 1
                    dst = pl.ds((t >> log_nb) + d * q0, bq, stride=d)
                    out_s[slot, dst, :] = acc / den
                    lse_s[slot, dst, :] = jnp.broadcast_to(mx + jnp.log(den), (bq, HEAD_DIM))
            return carry

        lax.fori_loop(0, seq_len // (bq * grp), group, 0)


def _att_group(seq_len):
    return max(ATT_GROUP_MIN, min(ATT_GROUP_MAX, seq_len // (2 * ATT_BLOCK)))


def _alibi_slopes(n):
    return jnp.asarray(2.0 ** (-8.0 * np.arange(1, n + 1) / n), F32)


def _attention(z_att, q_gain, k_gain, batch, seq_len, n_heads=N_HEADS):
    m = batch * seq_len
    assert (seq_len // ATT_BLOCK) % _att_group(seq_len) == 0
    assert seq_len // max(BRANCH_DILATIONS) >= ATT_BLOCK
    blk = (seq_len, HEAD_DIM)
    bias_shape = (len(BRANCH_DILATIONS), 3, ATT_BLOCK, ATT_BLOCK + 2 * BRANCH_HALF)
    blocks = 3 * _nbytes(blk, F32) + _nbytes(blk, BF16) + 2 * _nbytes((1, HEAD_DIM), F32)
    scratch = (5 * _nbytes(blk, F32) + 3 * _nbytes(blk, BF16) + _nbytes(bias_shape, F32)
               + 2 * _nbytes((2,) + blk, F32))

    def col(c):
        return lambda h, b: (b, h + c * n_heads)

    return pl.pallas_call(
        functools.partial(_attn_kernel, seq_len=seq_len),
        grid=(n_heads, batch),
        in_specs=[pl.BlockSpec(memory_space=pltpu.SMEM),
                  pl.BlockSpec(blk, col(0)), pl.BlockSpec(blk, col(1)), pl.BlockSpec(blk, col(2)),
                  pl.BlockSpec((1, HEAD_DIM), lambda h, b: (0, 0)),
                  pl.BlockSpec((1, HEAD_DIM), lambda h, b: (0, 0))],
        out_specs=pl.BlockSpec(blk, lambda h, b: (b, h)),
        out_shape=jax.ShapeDtypeStruct((m, n_heads * HEAD_DIM), BF16),
        scratch_shapes=[pltpu.VMEM(blk, F32)] * 5
                       + [pltpu.VMEM(blk, BF16)] * 3
                       + [pltpu.VMEM(bias_shape, F32),
                          pltpu.VMEM((2,) + blk, F32), pltpu.VMEM((2,) + blk, F32)],
        compiler_params=_params(blocks, scratch),
        name="dilated_attention",
    )(_alibi_slopes(n_heads), z_att, z_att, z_att,
      q_gain.reshape(1, -1), k_gain.reshape(1, -1))


def _residual_outputs(x_new, x_ref, xb_ref, ssq_ref):
    x_ref[...] = x_new
    xb_ref[...] = x_new.astype(BF16)
    part = _lane_partial_ssq(x_new)
    j = pl.program_id(1)

    @pl.when(j == 0)
    def _():
        ssq_ref[...] = part

    @pl.when(j > 0)
    def _():
        ssq_ref[...] += part


def _residual_specs(m, tm, tn):
    specs = [pl.BlockSpec((tm, tn), lambda i, j: (i, j)),
             pl.BlockSpec((tm, tn), lambda i, j: (i, j)),
             pl.BlockSpec((tm, LANES), lambda i, j: (i, 0))]
    shapes = [jax.ShapeDtypeStruct((m, D_MODEL), F32),
              jax.ShapeDtypeStruct((m, D_MODEL), BF16),
              jax.ShapeDtypeStruct((m, LANES), F32)]
    nbytes = _nbytes((tm, tn), F32) + _nbytes((tm, tn), BF16) + _nbytes((tm, LANES), F32)
    return specs, shapes, nbytes


def _mm_out_kernel(ret_ref, att_ref, wt_ref, wb_ref, x_ref, xo_ref, xb_ref, ssq_ref):
    acc = jnp.dot(ret_ref[...], wt_ref[...], preferred_element_type=F32)
    acc = acc + jnp.dot(att_ref[...], wb_ref[...], preferred_element_type=F32)
    _residual_outputs(x_ref[...] + acc, xo_ref, xb_ref, ssq_ref)


def _mm_out(ret, att, w, x, tm=1024, tn=512, jobs=()):
    m = x.shape[0]
    kh = MIX_HALF
    out_specs, out_shape, out_bytes = _residual_specs(m, tm, tn)
    blocks = (2 * _nbytes((tm, kh), BF16) + 2 * _nbytes((kh, tn), BF16) + _nbytes((tm, tn), F32)
              + out_bytes)
    return _mm_call(
        _mm_out_kernel,
        grid=(m // tm, D_MODEL // tn),
        in_specs=[pl.BlockSpec((tm, kh), lambda i, j: (i, 0)),
                  pl.BlockSpec((tm, kh), lambda i, j: (i, 0)),
                  pl.BlockSpec((kh, tn), lambda i, j: (0, j)),
                  pl.BlockSpec((kh, tn), lambda i, j: (1, j)),
                  pl.BlockSpec((tm, tn), lambda i, j: (i, j))],
        out_specs=out_specs, out_shape=out_shape,
        operands=[ret, att, w, w, x],
        block_bytes=blocks, temp_bytes=2 * _nbytes((tm, tn), F32), name="mm_out", jobs=jobs,
        deep=(2, 3))


def _mm_gateup_kernel(xb_ref, ssq_ref, wg_ref, wu_ref, h_ref):
    r = _row_scale(ssq_ref)
    xb = xb_ref[...]
    a = jnp.dot(xb, wg_ref[...], preferred_element_type=F32) * r
    u = jnp.dot(xb, wu_ref[...], preferred_element_type=F32) * r
    h_ref[...] = (a * _sigmoid(a) * u).astype(h_ref.dtype)


def _mm_gateup(xb, ssq, wg, wu, col0, ncols, tn, name, tm=1024, jobs=()):
    m, k = xb.shape
    s = ssq.shape[1]
    jb0 = col0 // tn
    blocks = (_nbytes((tm, k), BF16) + _nbytes((tm, s), F32) + 2 * _nbytes((k, tn), BF16)
              + _nbytes((tm, tn), BF16))
    (hid,), cast = _mm_call(
        _mm_gateup_kernel,
        grid=(m // tm, ncols // tn),
        in_specs=[pl.BlockSpec((tm, k), lambda i, j: (i, 0)),
                  pl.BlockSpec((tm, s), lambda i, j: (i, 0)),
                  pl.BlockSpec((k, tn), lambda i, j: (0, j + jb0)),
                  pl.BlockSpec((k, tn), lambda i, j: (0, j + jb0))],
        out_specs=[pl.BlockSpec((tm, tn), lambda i, j: (i, j))],
        out_shape=[jax.ShapeDtypeStruct((m, ncols), BF16)],
        operands=[xb, ssq, wg, wu],
        block_bytes=blocks, temp_bytes=3 * _nbytes((tm, tn), F32), name=name, jobs=jobs)
    return hid, cast


def _ff_split(d_ff):
    main = d_ff // FF_TILE * FF_TILE
    return main, d_ff - main


def _mm_down_kernel(hm_ref, ht_ref, wm_ref, wt_ref, x_ref, xo_ref, xb_ref, ssq_ref):
    acc = jnp.dot(hm_ref[...], wm_ref[...], preferred_element_type=F32)
    acc = acc + jnp.dot(ht_ref[...], wt_ref[...], preferred_element_type=F32)
    _residual_outputs(x_ref[...] + acc, xo_ref, xb_ref, ssq_ref)


def _mm_down(hid_main, hid_tail, w, x, tm=512, tn=512, jobs=()):
    m, k_main = hid_main.shape
    k_tail = hid_tail.shape[1]
    assert k_main % k_tail == 0
    tail_blk = k_main // k_tail
    out_specs, out_shape, out_bytes = _residual_specs(m, tm, tn)
    blocks = (_nbytes((tm, k_main + k_tail), BF16) + _nbytes((k_main + k_tail, tn), BF16)
              + _nbytes((tm, tn), F32) + out_bytes)
    return _mm_call(
        _mm_down_kernel,
        grid=(m // tm, D_MODEL // tn),
        in_specs=[pl.BlockSpec((tm, k_main), lambda i, j: (i, 0)),
                  pl.BlockSpec((tm, k_tail), lambda i, j: (i, 0)),
                  pl.BlockSpec((k_main, tn), lambda i, j: (0, j)),
                  pl.BlockSpec((k_tail, tn), lambda i, j: (tail_blk, j)),
                  pl.BlockSpec((tm, tn), lambda i, j: (i, j))],
        out_specs=out_specs, out_shape=out_shape,
        operands=[hid_main, hid_tail, w, w, x],
        block_bytes=blocks, temp_bytes=2 * _nbytes((tm, tn), F32), name="mm_down", jobs=jobs)


def _mm_ple_kernel(xb_ref, ssq_ref, wg_ref, p_ref, wp_ref, x_ref, xo_ref, xb_out_ref, ssq_out_ref):
    e = jnp.dot(xb_ref[...], wg_ref[...], preferred_element_type=F32) * _row_scale(ssq_ref)
    proj = jnp.dot(p_ref[...].astype(BF16), wp_ref[...], preferred_element_type=F32)
    _residual_outputs(x_ref[...] + _sigmoid(e) * proj, xo_ref, xb_out_ref, ssq_out_ref)


def _mm_ple(xb, ssq, wg, p, wp, x, tm=1024, tn=512, jobs=()):
    m, k = xb.shape
    s = ssq.shape[1]
    out_specs, out_shape, out_bytes = _residual_specs(m, tm, tn)
    blocks = (_nbytes((tm, k), BF16) + _nbytes((tm, s), F32) + _nbytes((k, tn), BF16)
              + _nbytes((tm, PLE_DIM), F32) + _nbytes((PLE_DIM, tn), BF16)
              + _nbytes((tm, tn), F32) + out_bytes)
    return _mm_call(
        _mm_ple_kernel,
        grid=(m // tm, D_MODEL // tn),
        in_specs=[pl.BlockSpec((tm, k), lambda i, j: (i, 0)),
                  pl.BlockSpec((tm, s), lambda i, j: (i, 0)),
                  pl.BlockSpec((k, tn), lambda i, j: (0, j)),
                  pl.BlockSpec((tm, PLE_DIM), lambda i, j: (i, 0)),
                  pl.BlockSpec((PLE_DIM, tn), lambda i, j: (0, j)),
                  pl.BlockSpec((tm, tn), lambda i, j: (i, j))],
        out_specs=out_specs, out_shape=out_shape,
        operands=[xb, ssq, wg, p, wp, x],
        block_bytes=blocks, temp_bytes=2 * _nbytes((tm, tn), F32), name="mm_ple", jobs=jobs,
        deep=(2,))


_HOSTED = {
    (0, "mm_in_ret"): [("w_out", 0), ("w_gate", 0)],
    (0, "mm_out"): [("w_up", 0), ("w_ple_gate", 0)],
    (0, "mm_gateup"): [("w_down", 0), ("w_in", 1)],
    (0, "mm_down"): [("w_out", 1)],
    (0, "mm_ple"): [("w_up", 1)],
    (1, "mm_in_ret"): [("w_gate", 1)],
    (1, "mm_gateup"): [("w_down", 1), ("w_ple_gate", 1), ("x_next", 0)],
}
_NEXT_X = "x_next"
_GAIN_OF = {"w_in": "ln_mix", "w_gate": "ln_ffn", "w_up": "ln_ffn", "w_ple_gate": "ln_ple"}
_ROW_BLOCK = {"w_in": 32, "w_out": 32, "w_gate": 64, "w_up": 64, "w_down": 128, "w_ple_gate": 32,
              "x_next": 64}
_ROW_BLOCK_OVERRIDE = {(0, "mm_in_ret", "w_out"): 64, (0, "mm_out", "w_ple_gate"): 64}


def _run_trunk(x, p, weights, raw, host, prepped=None):
    batch, seq_len, _ = x.shape
    m = batch * seq_len
    x = x.reshape(m, D_MODEL)
    xb, ssq = _prep(x) if prepped is None else prepped
    next_prep = []
    ff_main, ff_tail = _ff_split(raw["w_gate"].shape[2])
    assert ff_tail > 0 and ff_tail % LANES == 0

    def jobs_for(layer, call):
        if not host:
            return [], []
        names = _HOSTED.get((layer, call), [])
        jobs = []
        for name, wl in names:
            gain = raw[_GAIN_OF[name]][..., None] if name in _GAIN_OF else None
            rb = _ROW_BLOCK_OVERRIDE.get((layer, call, name), _ROW_BLOCK[name])
            jobs.append((raw[name], wl, gain, rb, name == _NEXT_X))
        return names, jobs

    def store(names, cast):
        cast = list(cast)
        for name, wl in names:
            if name == _NEXT_X:
                next_prep.extend([cast.pop(0), cast.pop(0)])
            else:
                weights[wl][name] = cast.pop(0)

    for i in range(len(weights)):
        lw = weights[i]
        names, jobs = jobs_for(i, "mm_in_ret")
        z_ret, cast = _mm_in(xb, ssq, lw["w_in"], 0, 4 * MIX_HALF, BF16, "mm_in_ret", tn=1024,
                             jobs=jobs)
        store(names, cast)
        names, jobs = jobs_for(i, "mm_in_att")
        z_att, cast = _mm_in(xb, ssq, lw["w_in"], 4 * MIX_HALF, 3 * MIX_HALF, F32, "mm_in_att",
                             tn=512 if jobs else 1024, jobs=jobs)
        store(names, cast)
        ret = _retention(z_ret, raw["ret_log_decay"][i], raw["ret_gn"][i], batch, seq_len)
        att = _attention(z_att, raw["q_norm"][i], raw["k_norm"][i], batch, seq_len)
        names, jobs = jobs_for(i, "mm_out")
        (x, xb, ssq), cast = _mm_out(ret, att, lw["w_out"], x, jobs=jobs)
        store(names, cast)
        names, jobs = jobs_for(i, "mm_gateup")
        hid_main, cast = _mm_gateup(xb, ssq, lw["w_gate"], lw["w_up"], 0, ff_main, FF_TILE,
                                    "mm_gateup", jobs=jobs)
        store(names, cast)
        hid_tail, _ = _mm_gateup(xb, ssq, lw["w_gate"], lw["w_up"], ff_main, ff_tail, ff_tail,
                                 "mm_gateup_tail")
        names, jobs = jobs_for(i, "mm_down")
        (x, xb, ssq), cast = _mm_down(hid_main, hid_tail, lw["w_down"], x, jobs=jobs)
        store(names, cast)
        names, jobs = jobs_for(i, "mm_ple")
        (x, xb, ssq), cast = _mm_ple(xb, ssq, lw["w_ple_gate"], p[i].reshape(m, PLE_DIM),
                                     lw["w_ple_proj"], x, jobs=jobs)
        store(names, cast)
    return x.reshape(batch, seq_len, D_MODEL), tuple(next_prep) or None


def kernel(x_prompt, x_sample, p_prompt, p_sample, ln_mix, w_in, ret_log_decay, ret_gn, q_norm,
           k_norm, w_out, ln_ffn, w_gate, w_up, w_down, ln_ple, w_ple_gate, w_ple_proj):
    depth = w_in.shape[0]
    raw = dict(ln_mix=ln_mix, w_in=w_in, ret_log_decay=ret_log_decay, ret_gn=ret_gn, q_norm=q_norm,
               k_norm=k_norm, w_out=w_out, ln_ffn=ln_ffn, w_gate=w_gate, w_up=w_up, w_down=w_down,
               ln_ple=ln_ple, w_ple_gate=w_ple_gate, w_ple_proj=w_ple_proj,
               x_next=x_sample.reshape(1, -1, D_MODEL))
    hosted = {(name, wl) for names in _HOSTED.values() for name, wl in names} if depth == 2 else set()
    weights = []
    for i in range(depth):
        lw = {"w_ple_proj": w_ple_proj[i].astype(BF16)}
        for name in ("w_in", "w_out", "w_gate", "w_up", "w_down", "w_ple_gate"):
            if (name, i) in hosted:
                continue
            w = raw[name][i]
            if name in _GAIN_OF:
                w = raw[_GAIN_OF[name]][i][:, None] * w
            lw[name] = w.astype(BF16)
        weights.append(lw)
    y_prompt, sample_prep = _run_trunk(x_prompt, p_prompt, weights, raw, host=depth == 2)
    y_sample, _ = _run_trunk(x_sample, p_sample, weights, raw, host=False, prepped=sample_prep)
    return (y_prompt, y_sample)
```
